```python
import jax, jax.numpy as jnp
from jax import lax
import numpy as np

D_MODEL = 1024
BATCH = 8
SEQ = 8192
DEPTH = 4
DEC_BATCH = 8
DEC_SEQ = 4096
PAST_LEN = 128

D_MIX = D_MODEL
D_A = D_MIX // 2
D_B = D_MIX - D_A
N_LRU_BLOCKS = 8
LRU_BW = D_A // N_LRU_BLOCKS
N_SC_GROUPS = 8
LRU_CONV_W = 4
LRU_CONV_PAD = (2, 1)
SC_CONV_W = 3
SC_CONV_PAD = (1, 1)
LRU_C = 8.0
D_FF = 2816
P_IN = 2 * D_A + 3 * D_B
EPS = 1e-6

kernel_name = "hymba_style_rglru_shortconv_macaron_encoder"


def rms_norm(x, g):
    x32 = x.astype(jnp.float32)
    y = x32 * lax.rsqrt(jnp.mean(x32 * x32, axis=-1, keepdims=True) + EPS)
    return (y * g.astype(jnp.float32)).astype(x.dtype)


def swiglu(x, w_gate, w_up, w_down):
    return (jax.nn.silu(x @ w_gate) * (x @ w_up)) @ w_down


def dw_conv(x, w, pad):
    c = x.shape[-1]
    return lax.conv_general_dilated(
        x, w[:, None, :].astype(x.dtype), window_strides=(1,), padding=[pad],
        dimension_numbers=("NWC", "WIO", "NWC"), feature_group_count=c)


def block_diag_linear(x, w, b):
    bsz, t, _ = x.shape
    xb = x.reshape(bsz, t, N_LRU_BLOCKS, LRU_BW)
    y = jnp.einsum("btnk,nkj->btnj", xb, w)
    return y.reshape(bsz, t, D_A) + b


def rglru_direction(xc, w_r, b_r, w_i, b_i, lam, reverse):
    r = jax.nn.sigmoid(block_diag_linear(xc, w_r.astype(jnp.float32), b_r.astype(jnp.float32)))
    i = jax.nn.sigmoid(block_diag_linear(xc, w_i.astype(jnp.float32), b_i.astype(jnp.float32)))
    log_a = -LRU_C * r * jax.nn.softplus(-lam.astype(jnp.float32))
    a = jnp.exp(log_a)
    u = jnp.sqrt(-jnp.expm1(2.0 * log_a)) * (i * xc)

    def step(h, inp):
        a_t, u_t = inp
        h = a_t * h + u_t
        return h, h

    h0 = jnp.zeros((xc.shape[0], xc.shape[2]), jnp.float32)
    _, hs = lax.scan(step, h0, (jnp.swapaxes(a, 0, 1), jnp.swapaxes(u, 0, 1)), reverse=reverse)
    return jnp.swapaxes(hs, 0, 1)


def token_mixer(x, g, w_in, lru_conv_w, lru_conv_b, lru_w_r, lru_b_r, lru_w_i, lru_b_i,
                lru_lambda, sc_conv_w, w_out):
    h = rms_norm(x, g)
    p = h @ w_in
    x_a, g_a, b_g, c_g, x_b = jnp.split(
        p, [D_A, 2 * D_A, 2 * D_A + D_B, 2 * D_A + 2 * D_B], axis=-1)
    xc = (dw_conv(x_a, lru_conv_w, LRU_CONV_PAD) + lru_conv_b).astype(jnp.float32)
    h_fwd = rglru_direction(xc, lru_w_r[0], lru_b_r[0], lru_w_i[0], lru_b_i[0], lru_lambda[0], False)
    h_bwd = rglru_direction(xc, lru_w_r[1], lru_b_r[1], lru_w_i[1], lru_b_i[1], lru_lambda[1], True)
    y_a = ((h_fwd + h_bwd) * jax.nn.gelu(g_a.astype(jnp.float32))).astype(x.dtype)
    y_b = b_g * dw_conv(c_g * x_b, sc_conv_w, SC_CONV_PAD)
    return jnp.concatenate([y_a, y_b], axis=-1) @ w_out


def trunk(x, ffn1_norm, ffn1_w_gate, ffn1_w_up, ffn1_w_down, mix_norm, w_in, lru_conv_w,
          lru_conv_b, lru_w_r, lru_b_r, lru_w_i, lru_b_i, lru_lambda, sc_conv_w, w_out,
          ffn2_norm, ffn2_w_gate, ffn2_w_up, ffn2_w_down, final_norm):
    for l in range(DEPTH):
        x = x + 0.5 * swiglu(rms_norm(x, ffn1_norm[l]), ffn1_w_gate[l], ffn1_w_up[l], ffn1_w_down[l])
        x = x + token_mixer(x, mix_norm[l], w_in[l], lru_conv_w[l], lru_conv_b[l], lru_w_r[l],
                            lru_b_r[l], lru_w_i[l], lru_b_i[l], lru_lambda[l], sc_conv_w[l], w_out[l])
        x = x + 0.5 * swiglu(rms_norm(x, ffn2_norm[l]), ffn2_w_gate[l], ffn2_w_up[l], ffn2_w_down[l])
    return rms_norm(x, final_norm)


def setup_inputs(seed: int = 0) -> dict:
    key = jax.random.key(seed)
    ks = jax.random.split(key, 24)
    f32 = jnp.float32

    def nrm(k, shape, fan_in):
        return jax.random.normal(k, shape, f32) * (fan_in ** -0.5)

    def gain(k, shape):
        return 1.0 + 0.02 * jax.random.normal(k, shape, f32)

    def bias(k, shape):
        return 0.02 * jax.random.normal(k, shape, f32)

    u = jax.random.uniform(ks[12], (DEPTH, 2, D_A), f32, 0.9, 0.999)
    s = u ** (1.0 / LRU_C)
    lru_lambda = jnp.log(s) - jnp.log1p(-s)

    return {
        "x_prompt": jax.random.normal(ks[0], (BATCH, SEQ, D_MODEL), f32),
        "x_sample": jax.random.normal(ks[1], (DEC_BATCH, DEC_SEQ, D_MODEL), f32),
        "ffn1_norm": gain(ks[2], (DEPTH, D_MODEL)),
        "ffn1_w_gate": nrm(ks[3], (DEPTH, D_MODEL, D_FF), D_MODEL),
        "ffn1_w_up": nrm(ks[4], (DEPTH, D_MODEL, D_FF), D_MODEL),
        "ffn1_w_down": nrm(ks[5], (DEPTH, D_FF, D_MODEL), D_FF),
        "mix_norm": gain(ks[6], (DEPTH, D_MODEL)),
        "w_in": nrm(ks[7], (DEPTH, D_MODEL, P_IN), D_MODEL),
        "lru_conv_w": nrm(ks[8], (DEPTH, LRU_CONV_W, D_A), LRU_CONV_W),
        "lru_conv_b": bias(ks[9], (DEPTH, D_A)),
        "lru_w_r": nrm(ks[10], (DEPTH, 2, N_LRU_BLOCKS, LRU_BW, LRU_BW), LRU_BW),
        "lru_b_r": bias(ks[11], (DEPTH, 2, D_A)),
        "lru_w_i": nrm(ks[13], (DEPTH, 2, N_LRU_BLOCKS, LRU_BW, LRU_BW), LRU_BW),
        "lru_b_i": bias(ks[14], (DEPTH, 2, D_A)),
        "lru_lambda": lru_lambda,
        "sc_conv_w": nrm(ks[15], (DEPTH, SC_CONV_W, D_B), SC_CONV_W),
        "w_out": nrm(ks[16], (DEPTH, D_MIX, D_MODEL), D_MIX),
        "ffn2_norm": gain(ks[17], (DEPTH, D_MODEL)),
        "ffn2_w_gate": nrm(ks[18], (DEPTH, D_MODEL, D_FF), D_MODEL),
        "ffn2_w_up": nrm(ks[19], (DEPTH, D_MODEL, D_FF), D_MODEL),
        "ffn2_w_down": nrm(ks[20], (DEPTH, D_FF, D_MODEL), D_FF),
        "final_norm": gain(ks[21], (D_MODEL,)),
    }


def reference(x_prompt, x_sample, ffn1_norm, ffn1_w_gate, ffn1_w_up, ffn1_w_down, mix_norm,
              w_in, lru_conv_w, lru_conv_b, lru_w_r, lru_b_r, lru_w_i, lru_b_i, lru_lambda,
              sc_conv_w, w_out, ffn2_norm, ffn2_w_gate, ffn2_w_up, ffn2_w_down, final_norm):
    y_prompt = trunk(x_prompt, ffn1_norm, ffn1_w_gate, ffn1_w_up, ffn1_w_down, mix_norm, w_in,
                     lru_conv_w, lru_conv_b, lru_w_r, lru_b_r, lru_w_i, lru_b_i, lru_lambda,
                     sc_conv_w, w_out, ffn2_norm, ffn2_w_gate, ffn2_w_up, ffn2_w_down, final_norm)
    y_sample = trunk(x_sample, ffn1_norm, ffn1_w_gate, ffn1_w_up, ffn1_w_down, mix_norm, w_in,
                     lru_conv_w, lru_conv_b, lru_w_r, lru_b_r, lru_w_i, lru_b_i, lru_lambda,
                     sc_conv_w, w_out, ffn2_norm, ffn2_w_gate, ffn2_w_up, ffn2_w_down, final_norm)
    return (y_prompt, y_sample)
```

```python
import functools

import jax
import jax.numpy as jnp
from jax import lax
from jax.experimental import pallas as pl
from jax.experimental.pallas import tpu as pltpu

D_MODEL = 1024
D_A = 512
D_B = 512
D_FF = 2816
P_IN = 2 * D_A + 3 * D_B
N_LRU_BLOCKS = 8
LRU_BW = 64
LRU_C = 8.0
EPS = 1e-6

ROWS_PER_STEP = 8
LRU_HALO_PREV = 2 * ROWS_PER_STEP
LRU_HALO_NEXT = 1 * ROWS_PER_STEP
GATE_TILE = 256

FFN_ROWS = 512
FFN_CHUNK = 256
MIX_ROWS = 512
SCAN_ROWS = 2048
SCAN_CHUNK = 256
VMEM_LIMIT = 56 * 1024 * 1024

F32 = jnp.float32
BF16 = jnp.bfloat16


def _rms(x, g):
    ms = jnp.mean(x * x, axis=-1, keepdims=True)
    return x * lax.rsqrt(ms + EPS) * g


def _sigmoid(x):
    return 0.5 * jnp.tanh(0.5 * x) + 0.5


def _const_spec(shape):
    nd = len(shape)
    return pl.BlockSpec(shape, lambda i: (0,) * nd, pipeline_mode=pl.Buffered(1))


def _ffn_kernel(x_ref, g_ref, wg_ref, wu_ref, wd_ref, *rest, final):
    if final:
        fg_ref, o_ref, act_ref = rest
    else:
        o_ref, act_ref = rest
    x = x_ref[...]
    hb = _rms(x, g_ref[...]).astype(BF16)
    for c in range(D_FF // FFN_CHUNK):
        cs = slice(c * FFN_CHUNK, (c + 1) * FFN_CHUNK)
        g = jnp.dot(hb, wg_ref[:, cs], preferred_element_type=F32)
        u = jnp.dot(hb, wu_ref[:, cs], preferred_element_type=F32)
        act_ref[:, cs] = ((g * _sigmoid(g)) * u).astype(BF16)
    y = jnp.dot(act_ref[...], wd_ref[...], preferred_element_type=F32)
    out = x + 0.5 * y
    if final:
        out = _rms(out, fg_ref[...])
    o_ref[...] = out


def _ffn(x, g, wg, wu, wd, final_g=None):
    n = x.shape[0]
    final = final_g is not None
    row = pl.BlockSpec((FFN_ROWS, D_MODEL), lambda i: (i, 0))
    in_specs = [row, _const_spec((1, D_MODEL)), _const_spec((D_MODEL, D_FF)),
                _const_spec((D_MODEL, D_FF)), _const_spec((D_FF, D_MODEL))]
    args = [x, g, wg, wu, wd]
    if final:
        in_specs.append(_const_spec((1, D_MODEL)))
        args.append(final_g)
    return pl.pallas_call(
        functools.partial(_ffn_kernel, final=final),
        out_shape=jax.ShapeDtypeStruct((n, D_MODEL), F32),
        grid=(n // FFN_ROWS,),
        in_specs=in_specs,
        out_specs=row,
        scratch_shapes=[pltpu.VMEM((FFN_ROWS, D_FF), BF16)],
        compiler_params=pltpu.CompilerParams(
            dimension_semantics=("arbitrary",), vmem_limit_bytes=VMEM_LIMIT),
        name="ffn_final" if final else "ffn",
    )(*args)


def _in_kernel(x_ref, g_ref, w_ref, *o_refs):
    hb = _rms(x_ref[...], g_ref[...]).astype(BF16)
    for k, o_ref in enumerate(o_refs):
        o_ref[...] = jnp.dot(hb, w_ref[:, k * D_A:(k + 1) * D_A], preferred_element_type=F32)


def _in_proj(x, g, w_in):
    n = x.shape[0]
    part = pl.BlockSpec((MIX_ROWS, D_A), lambda i: (i, 0))
    return pl.pallas_call(
        _in_kernel,
        out_shape=[jax.ShapeDtypeStruct((n, D_A), F32)] * 5,
        grid=(n // MIX_ROWS,),
        in_specs=[pl.BlockSpec((MIX_ROWS, D_MODEL), lambda i: (i, 0)),
                  _const_spec((1, D_MODEL)), _const_spec((D_MODEL, P_IN))],
        out_specs=[part] * 5,
        compiler_params=pltpu.CompilerParams(
            dimension_semantics=("arbitrary",), vmem_limit_bytes=VMEM_LIMIT),
        name="in_proj",
    )(x, g, w_in)


def _scan_kernel(xp_ref, xm_ref, xn_ref, cw_ref, cb_ref, w_ref, b_ref, lam_ref,
                 o_ref, xe_ref, a_ref, u_ref, h_ref, *, reverse, nblk):
    i = pl.program_id(0)
    blk = (nblk - 1 - i) if reverse else i

    @pl.when(i == 0)
    def _():
        h_ref[...] = jnp.zeros_like(h_ref)

    xe_ref[0:LRU_HALO_PREV] = jnp.where(blk == 0, 0.0, xp_ref[...])
    xe_ref[LRU_HALO_PREV:LRU_HALO_PREV + SCAN_ROWS] = xm_ref[...]
    xe_ref[LRU_HALO_PREV + SCAN_ROWS:] = jnp.where(blk == nblk - 1, 0.0, xn_ref[...])

    y = -lam_ref[...]
    softplus = jnp.maximum(y, 0.0) + jnp.log1p(jnp.exp(-jnp.abs(y)))
    decay = -LRU_C * softplus

    def gate_chunk(j, carry):
        r0 = pl.multiple_of(j * SCAN_CHUNK, SCAN_CHUNK)
        xc = cb_ref[...] + cw_ref[0:1] * xe_ref[pl.ds(r0, SCAN_CHUNK), :]
        for k in range(1, 4):
            xc = xc + cw_ref[k:k + 1] * xe_ref[pl.ds(r0 + k * ROWS_PER_STEP, SCAN_CHUNK), :]
        xcb = xc.astype(BF16)
        for h in range(D_A // GATE_TILE):
            hs = slice(h * GATE_TILE, (h + 1) * GATE_TILE)
            z = jnp.dot(xcb[:, hs], w_ref[h], preferred_element_type=F32)
            r = _sigmoid(z[:, :GATE_TILE] + b_ref[0:1, hs])
            ig = _sigmoid(z[:, GATE_TILE:] + b_ref[1:2, hs])
            th = jnp.tanh(0.5 * (r * decay[:, hs]))
            q = 1.0 / (1.0 - th)
            a_ref[pl.ds(r0, SCAN_CHUNK), hs] = (1.0 + th) * q
            u_ref[pl.ds(r0, SCAN_CHUNK), hs] = (2.0 * q * jnp.sqrt(-th)) * (ig * xc[:, hs])
        return carry

    lax.fori_loop(0, SCAN_ROWS // SCAN_CHUNK, gate_chunk, 0)

    nt = SCAN_ROWS // ROWS_PER_STEP

    def step(s, h):
        t = (nt - 1 - s) if reverse else s
        r = pl.multiple_of(t * ROWS_PER_STEP, ROWS_PER_STEP)
        h = a_ref[pl.ds(r, ROWS_PER_STEP), :] * h + u_ref[pl.ds(r, ROWS_PER_STEP), :]
        o_ref[pl.ds(r, ROWS_PER_STEP), :] = h
        return h

    h_ref[...] = lax.fori_loop(0, nt, step, h_ref[...], unroll=8)


def _lru_scan(xa, conv_w, conv_b, w_gate, b_gate, lam, reverse):
    n = xa.shape[0]
    nblk = n // SCAN_ROWS
    n_prev = SCAN_ROWS // LRU_HALO_PREV
    n_next = SCAN_ROWS // LRU_HALO_NEXT
    last_next = n // LRU_HALO_NEXT - 1

    def tblk(i):
        return (nblk - 1 - i) if reverse else i

    return pl.pallas_call(
        functools.partial(_scan_kernel, reverse=reverse, nblk=nblk),
        out_shape=jax.ShapeDtypeStruct((n, D_A), F32),
        grid=(nblk,),
        in_specs=[
            pl.BlockSpec((LRU_HALO_PREV, D_A), lambda i: (jnp.maximum(tblk(i) * n_prev - 1, 0), 0)),
            pl.BlockSpec((SCAN_ROWS, D_A), lambda i: (tblk(i), 0)),
            pl.BlockSpec((LRU_HALO_NEXT, D_A),
                         lambda i: (jnp.minimum((tblk(i) + 1) * n_next, last_next), 0)),
            _const_spec((4, D_A)), _const_spec((1, D_A)),
            _const_spec((D_A // GATE_TILE, GATE_TILE, 2 * GATE_TILE)),
            _const_spec((2, D_A)), _const_spec((1, D_A)),
        ],
        out_specs=pl.BlockSpec((SCAN_ROWS, D_A), lambda i: (tblk(i), 0)),
        scratch_shapes=[
            pltpu.VMEM((LRU_HALO_PREV + SCAN_ROWS + LRU_HALO_NEXT, D_A), F32),
            pltpu.VMEM((SCAN_ROWS, D_A), F32),
            pltpu.VMEM((SCAN_ROWS, D_A), F32),
            pltpu.VMEM((ROWS_PER_STEP, D_A), F32),
        ],
        compiler_params=pltpu.CompilerParams(
            dimension_semantics=("arbitrary",), vmem_limit_bytes=VMEM_LIMIT),
        name="lru_bwd" if reverse else "lru_fwd",
    )(xa, xa, xa, conv_w, conv_b, w_gate, b_gate, lam)


def _out_kernel(x_ref, hf_ref, hb_ref, ga_ref, bg_ref, cp_ref, cm_ref, cn_ref,
                xp_ref, xm_ref, xn_ref, sw_ref, w_ref, o_ref, *, nblk):
    i = pl.program_id(0)
    ya = (hf_ref[...] + hb_ref[...]) * jax.nn.gelu(ga_ref[...])

    cx = cm_ref[...] * xm_ref[...]
    cx_prev = jnp.where(i == 0, 0.0, cp_ref[...] * xp_ref[...])
    cx_next = jnp.where(i == nblk - 1, 0.0, cn_ref[...] * xn_ref[...])
    lo = jnp.concatenate([cx_prev, cx[:-ROWS_PER_STEP]], axis=0)
    hi = jnp.concatenate([cx[ROWS_PER_STEP:], cx_next], axis=0)
    yb = bg_ref[...] * (sw_ref[0:1] * lo + sw_ref[1:2] * cx + sw_ref[2:3] * hi)

    y = jnp.dot(ya.astype(BF16), w_ref[0:D_A], preferred_element_type=F32)
    y = y + jnp.dot(yb.astype(BF16), w_ref[D_A:], preferred_element_type=F32)
    o_ref[...] = x_ref[...] + y


def _out_proj(x, hf, hb, ga, bg, cg, xb, sc_w, w_out):
    n = x.shape[0]
    nblk = n // MIX_ROWS
    per = MIX_ROWS // ROWS_PER_STEP
    last = n // ROWS_PER_STEP - 1
    row = pl.BlockSpec((MIX_ROWS, D_MODEL), lambda i: (i, 0))
    part = pl.BlockSpec((MIX_ROWS, D_A), lambda i: (i, 0))
    prev = pl.BlockSpec((ROWS_PER_STEP, D_B), lambda i: (jnp.maximum(i * per - 1, 0), 0))
    nxt = pl.BlockSpec((ROWS_PER_STEP, D_B), lambda i: (jnp.minimum((i + 1) * per, last), 0))
    return pl.pallas_call(
        functools.partial(_out_kernel, nblk=nblk),
        out_shape=jax.ShapeDtypeStruct((n, D_MODEL), F32),
        grid=(nblk,),
        in_specs=[row, part, part, part, part, prev, part, nxt, prev, part, nxt,
                  _const_spec((3, D_B)), _const_spec((D_MODEL, D_MODEL))],
        out_specs=row,
        compiler_params=pltpu.CompilerParams(
            dimension_semantics=("arbitrary",), vmem_limit_bytes=VMEM_LIMIT),
        name="out_proj",
    )(x, hf, hb, ga, bg, cg, cg, cg, xb, xb, xb, sc_w, w_out)


def _gate_tiles(w_r, w_i):
    per = GATE_TILE // LRU_BW
    eye = jnp.eye(per, dtype=w_r.dtype)

    def tiles(w):
        w = w.reshape(D_A // GATE_TILE, per, LRU_BW, LRU_BW)
        return jnp.einsum("hnkj,nm->hnkmj", w, eye).reshape(D_A // GATE_TILE, GATE_TILE, GATE_TILE)

    return jnp.concatenate([tiles(w_r), tiles(w_i)], axis=-1).astype(BF16)


def _trunk(x, p):
    bsz, t, d = x.shape
    assert bsz == ROWS_PER_STEP and d == D_MODEL and (t * bsz) % SCAN_ROWS == 0
    n = t * bsz
    x = jnp.swapaxes(x, 0, 1).reshape(n, d)
    depth = p["w_in"].shape[0]
    for l in range(depth):
        x = _ffn(x, p["ffn1_norm"][l], p["ffn1_w_gate"][l], p["ffn1_w_up"][l], p["ffn1_w_down"][l])
        xa, ga, bg, cg, xb = _in_proj(x, p["mix_norm"][l], p["w_in"][l])
        hs = [
            _lru_scan(xa, p["lru_conv_w"][l], p["lru_conv_b"][l], p["gate_w"][l][dr],
                      p["gate_b"][l][dr], p["lru_lambda"][l][dr], reverse=bool(dr))
            for dr in range(2)
        ]
        x = _out_proj(x, hs[0], hs[1], ga, bg, cg, xb, p["sc_conv_w"][l], p["w_out"][l])
        final_g = p["final_norm"] if l == depth - 1 else None
        x = _ffn(x, p["ffn2_norm"][l], p["ffn2_w_gate"][l], p["ffn2_w_up"][l], p["ffn2_w_down"][l],
                 final_g)
    return jnp.swapaxes(x.reshape(t, bsz, d), 0, 1)


def kernel(x_prompt, x_sample, ffn1_norm, ffn1_w_gate, ffn1_w_up, ffn1_w_down, mix_norm, w_in, lru_conv_w, lru_conv_b, lru_w_r, lru_b_r, lru_w_i, lru_b_i, lru_lambda, sc_conv_w, w_out, ffn2_norm, ffn2_w_gate, ffn2_w_up, ffn2_w_down, final_norm):
    depth = w_in.shape[0]
    p = dict(
        ffn1_norm=ffn1_norm.reshape(depth, 1, D_MODEL),
        ffn1_w_gate=ffn1_w_gate.astype(BF16), ffn1_w_up=ffn1_w_up.astype(BF16),
        ffn1_w_down=ffn1_w_down.astype(BF16),
        mix_norm=mix_norm.reshape(depth, 1, D_MODEL),
        w_in=w_in.astype(BF16),
        lru_conv_w=lru_conv_w, lru_conv_b=lru_conv_b.reshape(depth, 1, D_A),
        gate_w=jax.vmap(jax.vmap(_gate_tiles))(lru_w_r, lru_w_i),
        gate_b=jnp.stack([lru_b_r, lru_b_i], axis=2),
        lru_lambda=lru_lambda.reshape(depth, 2, 1, D_A),
        sc_conv_w=sc_conv_w, w_out=w_out.astype(BF16),
        ffn2_norm=ffn2_norm.reshape(depth, 1, D_MODEL),
        ffn2_w_gate=ffn2_w_gate.astype(BF16), ffn2_w_up=ffn2_w_up.astype(BF16),
        ffn2_w_down=ffn2_w_down.astype(BF16),
        final_norm=final_norm.reshape(1, D_MODEL),
    )
    return (_trunk(x_prompt, p), _trunk(x_sample, p))
```

```python
import functools

import jax
import jax.numpy as jnp
from jax import lax
from jax.experimental import pallas as pl
from jax.experimental.pallas import tpu as pltpu

D_MODEL = 1024
D_A = 512
D_B = 512
D_FF = 2816
P_IN = 2 * D_A + 3 * D_B
N_LRU_BLOCKS = 8
LRU_BW = 64
LRU_C = 8.0
EPS = 1e-6

LANES = 128
ROWS_PER_STEP = 8
HALO = 2 * ROWS_PER_STEP
GATE_TILE = 256

FFN_ROWS = 512
FFN_CHUNK = 256
MIX_ROWS = 1024
GATE_CHUNK = 256
VMEM_LIMIT = 56 * 1024 * 1024

F32 = jnp.float32
BF16 = jnp.bfloat16


def _rms(x, g):
    ms = jnp.mean(x * x, axis=-1, keepdims=True)
    return x * lax.rsqrt(ms + EPS) * g


def _const_spec(shape, index=None):
    nd = len(shape)
    index = (0,) * nd if index is None else index
    return pl.BlockSpec(shape, lambda i: index, pipeline_mode=pl.Buffered(1))


def _ffn_kernel(*refs, final, batch_major_in, batch_major_out):
    x_ref, g_ref, wg_ref, wu_ref, wd_ref = refs[:5]
    refs = refs[5:]
    if final:
        fg_ref, refs = refs[0], refs[1:]
    o_ref, act_ref = refs[:2]
    refs = refs[2:]
    steps = FFN_ROWS // ROWS_PER_STEP
    if batch_major_in:
        xs_ref, refs = refs[0], refs[1:]
        for b in range(ROWS_PER_STEP):
            for j in range(D_MODEL // LANES):
                xs_ref[j, pl.ds(b, steps, stride=ROWS_PER_STEP), :] = x_ref[b, :, j * LANES:(j + 1) * LANES]
        x = jnp.concatenate([xs_ref[j] for j in range(D_MODEL // LANES)], axis=1)
    else:
        x = x_ref[...]
    hb = _rms(x, g_ref[...]).astype(BF16)
    for c in range(D_FF // FFN_CHUNK):
        cs = slice(c * FFN_CHUNK, (c + 1) * FFN_CHUNK)
        g = jnp.dot(hb, wg_ref[:, cs], preferred_element_type=F32)
        u = jnp.dot(hb, wu_ref[:, cs], preferred_element_type=F32)
        act_ref[:, cs] = ((g * (0.5 * jnp.tanh(0.5 * g) + 0.5)) * u).astype(BF16)
    y = jnp.dot(act_ref[...], wd_ref[...], preferred_element_type=F32)
    out = x + 0.5 * y
    if final:
        out = _rms(out, fg_ref[...])
    if batch_major_out:
        os_ref = refs[0]
        for j in range(D_MODEL // LANES):
            os_ref[j] = out[:, j * LANES:(j + 1) * LANES]
        for b in range(ROWS_PER_STEP):
            for j in range(D_MODEL // LANES):
                o_ref[b, :, j * LANES:(j + 1) * LANES] = os_ref[j, pl.ds(b, steps, stride=ROWS_PER_STEP), :]
    else:
        o_ref[...] = out


def _ffn(x, g, wg, wu, wd, final_g=None, batch_major_in=False, batch_major_out=False):
    steps = FFN_ROWS // ROWS_PER_STEP
    if batch_major_in:
        bsz, t, _ = x.shape
        n = bsz * t
    else:
        n = x.shape[0]
        bsz, t = ROWS_PER_STEP, n // ROWS_PER_STEP
    final = final_g is not None
    rows = pl.BlockSpec((FFN_ROWS, D_MODEL), lambda i: (i, 0))
    slab = pl.BlockSpec((bsz, steps, D_MODEL), lambda i: (0, i, 0))
    in_specs = [slab if batch_major_in else rows,
                _const_spec((1, D_MODEL)), _const_spec((D_MODEL, D_FF)),
                _const_spec((D_MODEL, D_FF)), _const_spec((D_FF, D_MODEL))]
    args = [x, g, wg, wu, wd]
    if final:
        in_specs.append(_const_spec((1, D_MODEL)))
        args.append(final_g)
    scratch = [pltpu.VMEM((FFN_ROWS, D_FF), BF16)]
    scratch += ([pltpu.VMEM((D_MODEL // LANES, FFN_ROWS, LANES), F32)]
                * (int(batch_major_in) + int(batch_major_out)))
    out_shape = (bsz, t, D_MODEL) if batch_major_out else (n, D_MODEL)
    return pl.pallas_call(
        functools.partial(_ffn_kernel, final=final, batch_major_in=batch_major_in,
                          batch_major_out=batch_major_out),
        out_shape=jax.ShapeDtypeStruct(out_shape, F32),
        grid=(n // FFN_ROWS,),
        in_specs=in_specs,
        out_specs=slab if batch_major_out else rows,
        scratch_shapes=scratch,
        compiler_params=pltpu.CompilerParams(
            dimension_semantics=("arbitrary",), vmem_limit_bytes=VMEM_LIMIT),
        name="ffn",
    )(*args)


def _normed_block(xp_ref, xm_ref, xn_ref, g_ref, blk, nblk):
    g = g_ref[...]
    hp = _rms(jnp.where(blk == 0, 0.0, xp_ref[...]), g).astype(BF16)
    hm = _rms(xm_ref[...], g).astype(BF16)
    hn = _rms(jnp.where(blk == nblk - 1, 0.0, xn_ref[...]), g).astype(BF16)
    return hm, jnp.concatenate([hp, hm, hn], axis=0)


def _quarter_decay(lam_ref):
    y = -lam_ref[...]
    softplus = jnp.maximum(y, 0.0) + jnp.log1p(jnp.exp(-jnp.abs(y)))
    return (0.25 * LRU_C) * softplus


def _lru_gates(xc_ref, w_ref, b_ref, qd, a_ref, u_ref):
    for c in range(MIX_ROWS // GATE_CHUNK):
        rs = slice(c * GATE_CHUNK, (c + 1) * GATE_CHUNK)
        for h in range(D_A // GATE_TILE):
            hs = slice(h * GATE_TILE, (h + 1) * GATE_TILE)
            xc = xc_ref[rs, hs]
            z = jnp.dot(xc.astype(BF16), w_ref[h], preferred_element_type=F32)
            t_r = jnp.tanh(z[:, :GATE_TILE] + b_ref[0:1, hs])
            t_i = jnp.tanh(z[:, GATE_TILE:] + b_ref[1:2, hs])
            n = jnp.tanh(t_r * qd[:, hs] + qd[:, hs])
            q = 1.0 / (1.0 + n)
            root = jnp.where(n == 0.0, 0.0, n * lax.rsqrt(n))
            a_ref[rs, hs] = (1.0 - n) * q
            u_ref[rs, hs] = (q * root) * ((t_i + 1.0) * xc)


def _lru_scan(a_ref, u_ref, h_ref, o_ref, reverse):
    nt = MIX_ROWS // ROWS_PER_STEP

    def step(s, h):
        t = (nt - 1 - s) if reverse else s
        r = pl.multiple_of(t * ROWS_PER_STEP, ROWS_PER_STEP)
        h = a_ref[pl.ds(r, ROWS_PER_STEP), :] * h + u_ref[pl.ds(r, ROWS_PER_STEP), :]
        o_ref[pl.ds(r, ROWS_PER_STEP), :] = h
        return h

    h_ref[...] = lax.fori_loop(0, nt, step, h_ref[...], unroll=8)


def _mix_bwd_kernel(xp_ref, xm_ref, xn_ref, g_ref, wxa_ref, cw_ref, cb_ref, gw_ref, gb_ref,
                    lam_ref, xc_ref, hb_ref, xa_s, a_s, u_s, h_s, *, nblk):
    i = pl.program_id(0)
    blk = nblk - 1 - i

    @pl.when(i == 0)
    def _():
        h_s[...] = jnp.zeros_like(h_s)

    _, he = _normed_block(xp_ref, xm_ref, xn_ref, g_ref, blk, nblk)
    xa_s[...] = jnp.dot(he, wxa_ref[...], preferred_element_type=F32)
    xc = cb_ref[...] + cw_ref[0:1] * xa_s[0:MIX_ROWS]
    for k in range(1, 4):
        xc = xc + cw_ref[k:k + 1] * xa_s[k * ROWS_PER_STEP:k * ROWS_PER_STEP + MIX_ROWS]
    xc_ref[...] = xc
    _lru_gates(xc_ref, gw_ref, gb_ref, _quarter_decay(lam_ref), a_s, u_s)
    _lru_scan(a_s, u_s, h_s, hb_ref, reverse=True)


def _halo_specs(n, reverse):
    nblk = n // MIX_ROWS
    per = MIX_ROWS // HALO
    last = n // HALO - 1

    def tblk(i):
        return (nblk - 1 - i) if reverse else i

    prev = pl.BlockSpec((HALO, D_MODEL), lambda i: (jnp.maximum(tblk(i) * per - 1, 0), 0))
    main = pl.BlockSpec((MIX_ROWS, D_MODEL), lambda i: (tblk(i), 0))
    nxt = pl.BlockSpec((HALO, D_MODEL), lambda i: (jnp.minimum((tblk(i) + 1) * per, last), 0))
    part = pl.BlockSpec((MIX_ROWS, D_A), lambda i: (tblk(i), 0))
    return nblk, prev, main, nxt, part


def _mix_bwd(x, g, w_in, conv_w, conv_b, gate_w, gate_b, lam):
    n = x.shape[0]
    nblk, prev, main, nxt, part = _halo_specs(n, reverse=True)
    return pl.pallas_call(
        functools.partial(_mix_bwd_kernel, nblk=nblk),
        out_shape=[jax.ShapeDtypeStruct((n, D_A), F32)] * 2,
        grid=(nblk,),
        in_specs=[prev, main, nxt, _const_spec((1, D_MODEL)),
                  _const_spec((D_MODEL, D_A)),
                  _const_spec((4, D_A)), _const_spec((1, D_A)),
                  _const_spec((D_A // GATE_TILE, GATE_TILE, 2 * GATE_TILE)),
                  _const_spec((2, D_A)), _const_spec((1, D_A))],
        out_specs=[part, part],
        scratch_shapes=[
            pltpu.VMEM((MIX_ROWS + 2 * HALO, D_A), F32),
            pltpu.VMEM((MIX_ROWS, D_A), F32),
            pltpu.VMEM((MIX_ROWS, D_A), F32),
            pltpu.VMEM((ROWS_PER_STEP, D_A), F32),
        ],
        compiler_params=pltpu.CompilerParams(
            dimension_semantics=("arbitrary",), vmem_limit_bytes=VMEM_LIMIT),
        name="mix_bwd",
    )(x, x, x, g, w_in, conv_w, conv_b, gate_w, gate_b, lam)


def _mix_fwd_kernel(xp_ref, xm_ref, xn_ref, g_ref, wga_ref, wbg_ref, wcg_ref, wxb_ref,
                    xc_ref, hb_ref, sw_ref, gw_ref, gb_ref, lam_ref, wo_ref,
                    o_ref, a_s, u_s, hf_s, gg_s, yb_s, h_s, *, nblk):
    i = pl.program_id(0)

    @pl.when(i == 0)
    def _():
        h_s[...] = jnp.zeros_like(h_s)

    hm, he = _normed_block(xp_ref, xm_ref, xn_ref, g_ref, i, nblk)
    gg_s[...] = jax.nn.gelu(jnp.dot(hm, wga_ref[...], preferred_element_type=F32))
    cx = (jnp.dot(he, wcg_ref[...], preferred_element_type=F32)
          * jnp.dot(he, wxb_ref[...], preferred_element_type=F32))
    lo = HALO - ROWS_PER_STEP
    conv = sw_ref[0:1] * cx[lo:lo + MIX_ROWS]
    for k in range(1, 3):
        conv = conv + sw_ref[k:k + 1] * cx[lo + k * ROWS_PER_STEP:lo + k * ROWS_PER_STEP + MIX_ROWS]
    yb_s[...] = (jnp.dot(hm, wbg_ref[...], preferred_element_type=F32) * conv).astype(BF16)

    _lru_gates(xc_ref, gw_ref, gb_ref, _quarter_decay(lam_ref), a_s, u_s)
    _lru_scan(a_s, u_s, h_s, hf_s, reverse=False)

    ya = ((hf_s[...] + hb_ref[...]) * gg_s[...]).astype(BF16)
    y = jnp.dot(ya, wo_ref[0:D_A], preferred_element_type=F32)
    y = y + jnp.dot(yb_s[...], wo_ref[D_A:], preferred_element_type=F32)
    o_ref[...] = xm_ref[...] + y


def _mix_fwd(x, xc, hb, g, w_in, sc_w, gate_w, gate_b, lam, w_out):
    n = x.shape[0]
    nblk, prev, main, nxt, part = _halo_specs(n, reverse=False)
    w_cols = [_const_spec((D_MODEL, D_A), (0, k)) for k in range(1, 5)]
    return pl.pallas_call(
        functools.partial(_mix_fwd_kernel, nblk=nblk),
        out_shape=jax.ShapeDtypeStruct((n, D_MODEL), F32),
        grid=(nblk,),
        in_specs=[prev, main, nxt, _const_spec((1, D_MODEL)), *w_cols,
                  part, part, _const_spec((3, D_B)),
                  _const_spec((D_A // GATE_TILE, GATE_TILE, 2 * GATE_TILE)),
                  _const_spec((2, D_A)), _const_spec((1, D_A)),
                  _const_spec((D_MODEL, D_MODEL))],
        out_specs=main,
        scratch_shapes=[
            pltpu.VMEM((MIX_ROWS, D_A), F32),
            pltpu.VMEM((MIX_ROWS, D_A), F32),
            pltpu.VMEM((MIX_ROWS, D_A), F32),
            pltpu.VMEM((MIX_ROWS, D_A), F32),
            pltpu.VMEM((MIX_ROWS, D_B), BF16),
            pltpu.VMEM((ROWS_PER_STEP, D_A), F32),
        ],
        compiler_params=pltpu.CompilerParams(
            dimension_semantics=("arbitrary",), vmem_limit_bytes=VMEM_LIMIT),
        name="mix_fwd",
    )(x, x, x, g, w_in, w_in, w_in, w_in, xc, hb, sc_w, gate_w, gate_b, lam, w_out)


def _gate_tiles(w_r, w_i):
    per = GATE_TILE // LRU_BW
    eye = jnp.eye(per, dtype=w_r.dtype)

    def tiles(w):
        w = w.reshape(D_A // GATE_TILE, per, LRU_BW, LRU_BW)
        return jnp.einsum("hnkj,nm->hnkmj", w, eye).reshape(D_A // GATE_TILE, GATE_TILE, GATE_TILE)

    return (0.5 * jnp.concatenate([tiles(w_r), tiles(w_i)], axis=-1)).astype(BF16)


def _trunk(x, p):
    bsz, t, d = x.shape
    assert bsz == ROWS_PER_STEP and d == D_MODEL and (t * bsz) % MIX_ROWS == 0
    depth = p["w_in"].shape[0]
    for l in range(depth):
        x = _ffn(x, p["ffn1_norm"][l], p["ffn1_w_gate"][l], p["ffn1_w_up"][l], p["ffn1_w_down"][l],
                 batch_major_in=(l == 0))
        xc, hb = _mix_bwd(x, p["mix_norm"][l], p["w_in"][l], p["lru_conv_w"][l], p["lru_conv_b"][l],
                          p["gate_w"][l][1], p["gate_b"][l][1], p["lru_lambda"][l][1])
        x = _mix_fwd(x, xc, hb, p["mix_norm"][l], p["w_in"][l], p["sc_conv_w"][l],
                     p["gate_w"][l][0], p["gate_b"][l][0], p["lru_lambda"][l][0], p["w_out"][l])
        last = l == depth - 1
        x = _ffn(x, p["ffn2_norm"][l], p["ffn2_w_gate"][l], p["ffn2_w_up"][l], p["ffn2_w_down"][l],
                 final_g=p["final_norm"] if last else None, batch_major_out=last)
    return x


def kernel(x_prompt, x_sample, ffn1_norm, ffn1_w_gate, ffn1_w_up, ffn1_w_down, mix_norm, w_in, lru_conv_w, lru_conv_b, lru_w_r, lru_b_r, lru_w_i, lru_b_i, lru_lambda, sc_conv_w, w_out, ffn2_norm, ffn2_w_gate, ffn2_w_up, ffn2_w_down, final_norm):
    depth = w_in.shape[0]
    p = dict(
        ffn1_norm=ffn1_norm.reshape(depth, 1, D_MODEL),
        ffn1_w_gate=ffn1_w_gate.astype(BF16), ffn1_w_up=ffn1_w_up.astype(BF16),
        ffn1_w_down=ffn1_w_down.astype(BF16),
        mix_norm=mix_norm.reshape(depth, 1, D_MODEL),
        w_in=w_in.astype(BF16),
        lru_conv_w=lru_conv_w, lru_conv_b=lru_conv_b.reshape(depth, 1, D_A),
        gate_w=jax.vmap(jax.vmap(_gate_tiles))(lru_w_r, lru_w_i),
        gate_b=0.5 * jnp.stack([lru_b_r, lru_b_i], axis=2),
        lru_lambda=lru_lambda.reshape(depth, 2, 1, D_A),
        sc_conv_w=sc_conv_w, w_out=w_out.astype(BF16),
        ffn2_norm=ffn2_norm.reshape(depth, 1, D_MODEL),
        ffn2_w_gate=ffn2_w_gate.astype(BF16), ffn2_w_up=ffn2_w_up.astype(BF16),
        ffn2_w_down=ffn2_w_down.astype(BF16),
        final_norm=final_norm.reshape(1, D_MODEL),
    )
    return (_trunk(x_prompt, p), _trunk(x_sample, p))
```

```python
import functools

import jax
import jax.numpy as jnp
from jax import lax
from jax.experimental import pallas as pl
from jax.experimental.pallas import tpu as pltpu

D_MODEL = 1024
D_A = 512
D_B = 512
D_FF = 2816
P_IN = 2 * D_A + 3 * D_B
N_LRU_BLOCKS = 8
LRU_BW = 64
LRU_C = 8.0
EPS = 1e-6

LANES = 128
ROWS_PER_STEP = 8
HALO = 2 * ROWS_PER_STEP
GATE_TILE = 256

FFN_ROWS = 512
FFN_CHUNK = 256
MIX_ROWS = 1024
GATE_CHUNK = 256
VMEM_LIMIT = 56 * 1024 * 1024

F32 = jnp.float32
BF16 = jnp.bfloat16


def _rms(x, g):
    ms = jnp.mean(x * x, axis=-1, keepdims=True)
    return x * lax.rsqrt(ms + EPS) * g


def _const_spec(shape, index=None):
    nd = len(shape)
    index = (0,) * nd if index is None else index
    return pl.BlockSpec(shape, lambda i: index, pipeline_mode=pl.Buffered(1))


def _to_time_major(x_ref, xs_ref):
    steps = FFN_ROWS // ROWS_PER_STEP
    for b in range(ROWS_PER_STEP):
        for j in range(D_MODEL // LANES):
            xs_ref[j, pl.ds(b, steps, stride=ROWS_PER_STEP), :] = x_ref[b, :, j * LANES:(j + 1) * LANES]
    return jnp.concatenate([xs_ref[j] for j in range(D_MODEL // LANES)], axis=1)


def _ffn_kernel(*refs, final, batch_major_in, batch_major_out):
    x_ref, xnext_ref, g_ref, wg_ref, wu_ref, wd_ref = refs[:6]
    refs = refs[6:]
    if final:
        fg_ref, refs = refs[0], refs[1:]
    o_ref, act_ref, hb_ref, act0_ref = refs[:4]
    refs = refs[4:]
    if batch_major_in:
        xs_ref, refs = refs[0], refs[1:]
        load = functools.partial(_to_time_major, xs_ref=xs_ref)
    else:
        load = lambda ref: ref[...]

    def hidden_chunk(c):
        cs = slice(c * FFN_CHUNK, (c + 1) * FFN_CHUNK)
        g = jnp.dot(hb_ref[...], wg_ref[:, cs], preferred_element_type=F32)
        u = jnp.dot(hb_ref[...], wu_ref[:, cs], preferred_element_type=F32)
        return ((g * (0.5 * jnp.tanh(0.5 * g) + 0.5)) * u).astype(BF16)

    def start_tile(ref):
        hb_ref[...] = _rms(load(ref), g_ref[...]).astype(BF16)
        act0_ref[...] = hidden_chunk(0)

    @pl.when(pl.program_id(0) == 0)
    def _():
        start_tile(x_ref)

    act_ref[:, 0:FFN_CHUNK] = act0_ref[...]
    for c in range(1, D_FF // FFN_CHUNK):
        act_ref[:, c * FFN_CHUNK:(c + 1) * FFN_CHUNK] = hidden_chunk(c)
    start_tile(xnext_ref)
    y = jnp.dot(act_ref[...], wd_ref[...], preferred_element_type=F32)
    out = load(x_ref) + 0.5 * y
    if final:
        out = _rms(out, fg_ref[...])
    if batch_major_out:
        os_ref = refs[0]
        steps = FFN_ROWS // ROWS_PER_STEP
        for j in range(D_MODEL // LANES):
            os_ref[j] = out[:, j * LANES:(j + 1) * LANES]
        for b in range(ROWS_PER_STEP):
            for j in range(D_MODEL // LANES):
                o_ref[b, :, j * LANES:(j + 1) * LANES] = os_ref[j, pl.ds(b, steps, stride=ROWS_PER_STEP), :]
    else:
        o_ref[...] = out


def _ffn(x, g, wg, wu, wd, final_g=None, batch_major_in=False, batch_major_out=False):
    steps = FFN_ROWS // ROWS_PER_STEP
    if batch_major_in:
        bsz, t, _ = x.shape
        n = bsz * t
    else:
        n = x.shape[0]
        bsz, t = ROWS_PER_STEP, n // ROWS_PER_STEP
    nblk = n // FFN_ROWS
    final = final_g is not None

    def tile_spec(batch_major, ahead):
        if batch_major:
            return pl.BlockSpec((bsz, steps, D_MODEL), lambda i: (0, jnp.minimum(i + ahead, nblk - 1), 0))
        return pl.BlockSpec((FFN_ROWS, D_MODEL), lambda i: (jnp.minimum(i + ahead, nblk - 1), 0))

    in_specs = [tile_spec(batch_major_in, 0), tile_spec(batch_major_in, 1),
                _const_spec((1, D_MODEL)), _const_spec((D_MODEL, D_FF)),
                _const_spec((D_MODEL, D_FF)), _const_spec((D_FF, D_MODEL))]
    args = [x, x, g, wg, wu, wd]
    if final:
        in_specs.append(_const_spec((1, D_MODEL)))
        args.append(final_g)
    scratch = [pltpu.VMEM((FFN_ROWS, D_FF), BF16), pltpu.VMEM((FFN_ROWS, D_MODEL), BF16),
               pltpu.VMEM((FFN_ROWS, FFN_CHUNK), BF16)]
    scratch += ([pltpu.VMEM((D_MODEL // LANES, FFN_ROWS, LANES), F32)]
                * (int(batch_major_in) + int(batch_major_out)))
    out_shape = (bsz, t, D_MODEL) if batch_major_out else (n, D_MODEL)
    return pl.pallas_call(
        functools.partial(_ffn_kernel, final=final, batch_major_in=batch_major_in,
                          batch_major_out=batch_major_out),
        out_shape=jax.ShapeDtypeStruct(out_shape, F32),
        grid=(nblk,),
        in_specs=in_specs,
        out_specs=tile_spec(batch_major_out, 0),
        scratch_shapes=scratch,
        compiler_params=pltpu.CompilerParams(
            dimension_semantics=("arbitrary",), vmem_limit_bytes=VMEM_LIMIT),
        name="ffn",
    )(*args)


def _normed_block(xp_ref, xm_ref, xn_ref, g_ref, blk, nblk):
    g = g_ref[...]
    hp = _rms(jnp.where(blk == 0, 0.0, xp_ref[...]), g).astype(BF16)
    hm = _rms(xm_ref[...], g).astype(BF16)
    hn = _rms(jnp.where(blk == nblk - 1, 0.0, xn_ref[...]), g).astype(BF16)
    return hm, jnp.concatenate([hp, hm, hn], axis=0)


def _quarter_decay(lam_ref):
    y = -lam_ref[...]
    softplus = jnp.maximum(y, 0.0) + jnp.log1p(jnp.exp(-jnp.abs(y)))
    return (0.25 * LRU_C) * softplus


def _lru_direction(xc_ref, w_ref, b_ref, qd, a_ref, u_ref, h_ref, o_ref, reverse):
    chunks = range(MIX_ROWS // GATE_CHUNK)
    steps = range(GATE_CHUNK // ROWS_PER_STEP)
    h_state = h_ref[...]
    for c in (reversed(chunks) if reverse else chunks):
        rs = slice(c * GATE_CHUNK, (c + 1) * GATE_CHUNK)
        for h in range(D_A // GATE_TILE):
            hs = slice(h * GATE_TILE, (h + 1) * GATE_TILE)
            xc = xc_ref[rs, hs]
            z = jnp.dot(xc.astype(BF16), w_ref[h], preferred_element_type=F32)
            t_r = jnp.tanh(z[:, :GATE_TILE] + b_ref[0:1, hs])
            t_i = jnp.tanh(z[:, GATE_TILE:] + b_ref[1:2, hs])
            n = jnp.tanh(t_r * qd[:, hs] + qd[:, hs])
            q = 1.0 / (1.0 + n)
            root = jnp.where(n == 0.0, 0.0, n * lax.rsqrt(n))
            a_ref[rs, hs] = (1.0 - n) * q
            u_ref[rs, hs] = (q * root) * ((t_i + 1.0) * xc)
        for s in (reversed(steps) if reverse else steps):
            ts = slice(c * GATE_CHUNK + s * ROWS_PER_STEP, c * GATE_CHUNK + (s + 1) * ROWS_PER_STEP)
            h_state = a_ref[ts, :] * h_state + u_ref[ts, :]
            o_ref[ts, :] = h_state
    h_ref[...] = h_state


def _mix_bwd_kernel(xp_ref, xm_ref, xn_ref, g_ref, wxa_ref, cw_ref, cb_ref, gw_ref, gb_ref,
                    lam_ref, xc_ref, hb_ref, xa_s, a_s, u_s, h_s, *, nblk):
    i = pl.program_id(0)
    blk = nblk - 1 - i

    @pl.when(i == 0)
    def _():
        h_s[...] = jnp.zeros_like(h_s)

    _, he = _normed_block(xp_ref, xm_ref, xn_ref, g_ref, blk, nblk)
    xa_s[...] = jnp.dot(he, wxa_ref[...], preferred_element_type=F32)
    xc = cb_ref[...] + cw_ref[0:1] * xa_s[0:MIX_ROWS]
    for k in range(1, 4):
        xc = xc + cw_ref[k:k + 1] * xa_s[k * ROWS_PER_STEP:k * ROWS_PER_STEP + MIX_ROWS]
    xc_ref[...] = xc
    _lru_direction(xc_ref, gw_ref, gb_ref, _quarter_decay(lam_ref), a_s, u_s, h_s, hb_ref, reverse=True)


def _halo_specs(n, tblk, **kwargs):
    per = MIX_ROWS // HALO
    last = n // HALO - 1
    prev = pl.BlockSpec((HALO, D_MODEL), lambda i: (jnp.maximum(tblk(i) * per - 1, 0), 0), **kwargs)
    main = pl.BlockSpec((MIX_ROWS, D_MODEL), lambda i: (tblk(i), 0), **kwargs)
    nxt = pl.BlockSpec((HALO, D_MODEL), lambda i: (jnp.minimum((tblk(i) + 1) * per, last), 0), **kwargs)
    return prev, main, nxt


def _mix_bwd(x, g, w_in, conv_w, conv_b, gate_w, gate_b, lam):
    n = x.shape[0]
    nblk = n // MIX_ROWS
    prev, main, nxt = _halo_specs(n, lambda i: nblk - 1 - i)
    part = pl.BlockSpec((MIX_ROWS, D_A), lambda i: (nblk - 1 - i, 0))
    return pl.pallas_call(
        functools.partial(_mix_bwd_kernel, nblk=nblk),
        out_shape=[jax.ShapeDtypeStruct((n, D_A), F32)] * 2,
        grid=(nblk,),
        in_specs=[prev, main, nxt, _const_spec((1, D_MODEL)),
                  _const_spec((D_MODEL, D_A)),
                  _const_spec((4, D_A)), _const_spec((1, D_A)),
                  _const_spec((D_A // GATE_TILE, GATE_TILE, 2 * GATE_TILE)),
                  _const_spec((2, D_A)), _const_spec((1, D_A))],
        out_specs=[part, part],
        scratch_shapes=[
            pltpu.VMEM((MIX_ROWS + 2 * HALO, D_A), F32),
            pltpu.VMEM((MIX_ROWS, D_A), F32),
            pltpu.VMEM((MIX_ROWS, D_A), F32),
            pltpu.VMEM((ROWS_PER_STEP, D_A), F32),
        ],
        compiler_params=pltpu.CompilerParams(
            dimension_semantics=("arbitrary",), vmem_limit_bytes=VMEM_LIMIT),
        name="mix_bwd",
    )(x, x, x, g, w_in, conv_w, conv_b, gate_w, gate_b, lam)


def _mix_fwd_kernel(xp_ref, xm_ref, xn_ref, g_ref, wga_ref, wbg_ref, wcg_ref, wxb_ref,
                    xc_ref, hb_ref, sw_ref, gw_ref, gb_ref, lam_ref, wo_ref,
                    o_ref, a_s, u_s, hf_s, gg_s, yb_s, h_s, *, nblk):
    i = pl.program_id(0)

    @pl.when(i == 0)
    def _():
        h_s[...] = jnp.zeros_like(h_s)

    hm, he = _normed_block(xp_ref, xm_ref, xn_ref, g_ref, i, nblk)
    gg_s[...] = jax.nn.gelu(jnp.dot(hm, wga_ref[...], preferred_element_type=F32))
    cx = (jnp.dot(he, wcg_ref[...], preferred_element_type=F32)
          * jnp.dot(he, wxb_ref[...], preferred_element_type=F32))
    lo = HALO - ROWS_PER_STEP
    conv = sw_ref[0:1] * cx[lo:lo + MIX_ROWS]
    for k in range(1, 3):
        conv = conv + sw_ref[k:k + 1] * cx[lo + k * ROWS_PER_STEP:lo + k * ROWS_PER_STEP + MIX_ROWS]
    yb_s[...] = (jnp.dot(hm, wbg_ref[...], preferred_element_type=F32) * conv).astype(BF16)

    _lru_direction(xc_ref, gw_ref, gb_ref, _quarter_decay(lam_ref), a_s, u_s, h_s, hf_s, reverse=False)

    ya = ((hf_s[...] + hb_ref[...]) * gg_s[...]).astype(BF16)
    y = jnp.dot(ya, wo_ref[0:D_A], preferred_element_type=F32)
    y = y + jnp.dot(yb_s[...], wo_ref[D_A:], preferred_element_type=F32)
    o_ref[...] = xm_ref[...] + y


def _mix_fwd(x, xc, hb, g, w_in, sc_w, gate_w, gate_b, lam, w_out):
    n = x.shape[0]
    nblk = n // MIX_ROWS
    prev, main, nxt = _halo_specs(n, lambda i: i)
    part = pl.BlockSpec((MIX_ROWS, D_A), lambda i: (i, 0))
    w_cols = [_const_spec((D_MODEL, D_A), (0, k)) for k in range(1, 5)]
    return pl.pallas_call(
        functools.partial(_mix_fwd_kernel, nblk=nblk),
        out_shape=jax.ShapeDtypeStruct((n, D_MODEL), F32),
        grid=(nblk,),
        in_specs=[prev, main, nxt, _const_spec((1, D_MODEL)), *w_cols,
                  part, part, _const_spec((3, D_B)),
                  _const_spec((D_A // GATE_TILE, GATE_TILE, 2 * GATE_TILE)),
                  _const_spec((2, D_A)), _const_spec((1, D_A)),
                  _const_spec((D_MODEL, D_MODEL))],
        out_specs=main,
        scratch_shapes=[
            pltpu.VMEM((MIX_ROWS, D_A), F32),
            pltpu.VMEM((MIX_ROWS, D_A), F32),
            pltpu.VMEM((MIX_ROWS, D_A), F32),
            pltpu.VMEM((MIX_ROWS, D_A), F32),
            pltpu.VMEM((MIX_ROWS, D_B), BF16),
            pltpu.VMEM((ROWS_PER_STEP, D_A), F32),
        ],
        compiler_params=pltpu.CompilerParams(
            dimension_semantics=("arbitrary",), vmem_limit_bytes=VMEM_LIMIT),
        name="mix_fwd",
    )(x, x, x, g, w_in, w_in, w_in, w_in, xc, hb, sc_w, gate_w, gate_b, lam, w_out)


def _gate_tiles(w_r, w_i):
    per = GATE_TILE // LRU_BW
    eye = jnp.eye(per, dtype=w_r.dtype)

    def tiles(w):
        w = w.reshape(D_A // GATE_TILE, per, LRU_BW, LRU_BW)
        return jnp.einsum("hnkj,nm->hnkmj", w, eye).reshape(D_A // GATE_TILE, GATE_TILE, GATE_TILE)

    return (0.5 * jnp.concatenate([tiles(w_r), tiles(w_i)], axis=-1)).astype(BF16)


def _trunk(x, p):
    bsz, t, d = x.shape
    assert bsz == ROWS_PER_STEP and d == D_MODEL and (t * bsz) % MIX_ROWS == 0
    depth = p["w_in"].shape[0]
    for l in range(depth):
        x = _ffn(x, p["ffn1_norm"][l], p["ffn1_w_gate"][l], p["ffn1_w_up"][l], p["ffn1_w_down"][l],
                 batch_major_in=(l == 0))
        xc, hb = _mix_bwd(x, p["mix_norm"][l], p["w_in"][l], p["lru_conv_w"][l], p["lru_conv_b"][l],
                          p["gate_w"][l][1], p["gate_b"][l][1], p["lru_lambda"][l][1])
        x = _mix_fwd(x, xc, hb, p["mix_norm"][l], p["w_in"][l], p["sc_conv_w"][l],
                     p["gate_w"][l][0], p["gate_b"][l][0], p["lru_lambda"][l][0], p["w_out"][l])
        last = l == depth - 1
        x = _ffn(x, p["ffn2_norm"][l], p["ffn2_w_gate"][l], p["ffn2_w_up"][l], p["ffn2_w_down"][l],
                 final_g=p["final_norm"] if last else None, batch_major_out=last)
    return x


def kernel(x_prompt, x_sample, ffn1_norm, ffn1_w_gate, ffn1_w_up, ffn1_w_down, mix_norm, w_in, lru_conv_w, lru_conv_b, lru_w_r, lru_b_r, lru_w_i, lru_b_i, lru_lambda, sc_conv_w, w_out, ffn2_norm, ffn2_w_gate, ffn2_w_up, ffn2_w_down, final_norm):
    depth = w_in.shape[0]
    p = dict(
        ffn1_norm=ffn1_norm.reshape(depth, 1, D_MODEL),
        ffn1_w_gate=ffn1_w_gate.astype(BF16), ffn1_w_up=ffn1_w_up.astype(BF16),
        ffn1_w_down=ffn1_w_down.astype(BF16),
        mix_norm=mix_norm.reshape(depth, 1, D_MODEL),
        w_in=w_in.astype(BF16),
        lru_conv_w=lru_conv_w, lru_conv_b=lru_conv_b.reshape(depth, 1, D_A),
        gate_w=jax.vmap(jax.vmap(_gate_tiles))(lru_w_r, lru_w_i),
        gate_b=0.5 * jnp.stack([lru_b_r, lru_b_i], axis=2),
        lru_lambda=lru_lambda.reshape(depth, 2, 1, D_A),
        sc_conv_w=sc_conv_w, w_out=w_out.astype(BF16),
        ffn2_norm=ffn2_norm.reshape(depth, 1, D_MODEL),
        ffn2_w_gate=ffn2_w_gate.astype(BF16), ffn2_w_up=ffn2_w_up.astype(BF16),
        ffn2_w_down=ffn2_w_down.astype(BF16),
        final_norm=final_norm.reshape(1, D_MODEL),
    )
    return (_trunk(x_prompt, p), _trunk(x_sample, p))
```

```python
import functools

import jax
import jax.numpy as jnp
from jax import lax
from jax.experimental import pallas as pl
from jax.experimental.pallas import tpu as pltpu

D_MODEL = 1024
D_A = 512
D_B = 512
D_FF = 2816
P_IN = 2 * D_A + 3 * D_B
N_LRU_BLOCKS = 8
LRU_BW = 64
LRU_C = 8.0
EPS = 1e-6

LANES = 128
ROWS_PER_STEP = 8
HALO = 2 * ROWS_PER_STEP
GATE_TILE = 256

FFN_ROWS = 1024
FFN_CHUNK = 256
MIX_ROWS = 1024
GATE_CHUNK = 256
DOT_ROWS = 512
VMEM_LIMIT = 56 * 1024 * 1024

F32 = jnp.float32
BF16 = jnp.bfloat16


def _rms(x, g):
    ms = jnp.mean(x * x, axis=-1, keepdims=True)
    return x * lax.rsqrt(ms + EPS) * g


def _const_spec(shape, index=None):
    nd = len(shape)
    index = (0,) * nd if index is None else index
    return pl.BlockSpec(shape, lambda i: index, pipeline_mode=pl.Buffered(1))


def _to_time_major(x_ref, xs_ref):
    steps = FFN_ROWS // ROWS_PER_STEP
    for b in range(ROWS_PER_STEP):
        for j in range(D_MODEL // LANES):
            xs_ref[j, pl.ds(b, steps, stride=ROWS_PER_STEP), :] = x_ref[b, :, j * LANES:(j + 1) * LANES]
    return jnp.concatenate([xs_ref[j] for j in range(D_MODEL // LANES)], axis=1)


def _ffn_kernel(*refs, final, batch_major_in, batch_major_out):
    x_ref, xnext_ref, g_ref, wg_ref, wu_ref, wd_ref = refs[:6]
    refs = refs[6:]
    if final:
        fg_ref, refs = refs[0], refs[1:]
    o_ref, act_ref, hb_ref, act0_ref = refs[:4]
    refs = refs[4:]
    if batch_major_in:
        xs_ref, refs = refs[0], refs[1:]
        load = functools.partial(_to_time_major, xs_ref=xs_ref)
    else:
        load = lambda ref: ref[...]

    def hidden_chunk(c):
        cs = slice(c * FFN_CHUNK, (c + 1) * FFN_CHUNK)
        g = jnp.dot(hb_ref[...], wg_ref[:, cs], preferred_element_type=F32)
        u = jnp.dot(hb_ref[...], wu_ref[:, cs], preferred_element_type=F32)
        return ((g * (0.5 * jnp.tanh(0.5 * g) + 0.5)) * u).astype(BF16)

    def start_tile(ref):
        hb_ref[...] = _rms(load(ref), g_ref[...]).astype(BF16)
        act0_ref[...] = hidden_chunk(0)

    @pl.when(pl.program_id(0) == 0)
    def _():
        start_tile(x_ref)

    act_ref[:, 0:FFN_CHUNK] = act0_ref[...]
    for c in range(1, D_FF // FFN_CHUNK):
        act_ref[:, c * FFN_CHUNK:(c + 1) * FFN_CHUNK] = hidden_chunk(c)
    start_tile(xnext_ref)
    y = jnp.dot(act_ref[...], wd_ref[...], preferred_element_type=F32)
    out = load(x_ref) + 0.5 * y
    if final:
        out = _rms(out, fg_ref[...])
    if batch_major_out:
        os_ref = refs[0]
        steps = FFN_ROWS // ROWS_PER_STEP
        for j in range(D_MODEL // LANES):
            os_ref[j] = out[:, j * LANES:(j + 1) * LANES]
        for b in range(ROWS_PER_STEP):
            for j in range(D_MODEL // LANES):
                o_ref[b, :, j * LANES:(j + 1) * LANES] = os_ref[j, pl.ds(b, steps, stride=ROWS_PER_STEP), :]
    else:
        o_ref[...] = out


def _ffn(x, g, wg, wu, wd, final_g=None, batch_major_in=False, batch_major_out=False):
    steps = FFN_ROWS // ROWS_PER_STEP
    if batch_major_in:
        bsz, t, _ = x.shape
        n = bsz * t
    else:
        n = x.shape[0]
        bsz, t = ROWS_PER_STEP, n // ROWS_PER_STEP
    nblk = n // FFN_ROWS
    final = final_g is not None

    def tile_spec(batch_major, ahead):
        if batch_major:
            return pl.BlockSpec((bsz, steps, D_MODEL), lambda i: (0, jnp.minimum(i + ahead, nblk - 1), 0))
        return pl.BlockSpec((FFN_ROWS, D_MODEL), lambda i: (jnp.minimum(i + ahead, nblk - 1), 0))

    in_specs = [tile_spec(batch_major_in, 0), tile_spec(batch_major_in, 1),
                _const_spec((1, D_MODEL)), _const_spec((D_MODEL, D_FF)),
                _const_spec((D_MODEL, D_FF)), _const_spec((D_FF, D_MODEL))]
    args = [x, x, g, wg, wu, wd]
    if final:
        in_specs.append(_const_spec((1, D_MODEL)))
        args.append(final_g)
    scratch = [pltpu.VMEM((FFN_ROWS, D_FF), BF16), pltpu.VMEM((FFN_ROWS, D_MODEL), BF16),
               pltpu.VMEM((FFN_ROWS, FFN_CHUNK), BF16)]
    scratch += ([pltpu.VMEM((D_MODEL // LANES, FFN_ROWS, LANES), F32)]
                * (int(batch_major_in) + int(batch_major_out)))
    out_shape = (bsz, t, D_MODEL) if batch_major_out else (n, D_MODEL)
    return pl.pallas_call(
        functools.partial(_ffn_kernel, final=final, batch_major_in=batch_major_in,
                          batch_major_out=batch_major_out),
        out_shape=jax.ShapeDtypeStruct(out_shape, F32),
        grid=(nblk,),
        in_specs=in_specs,
        out_specs=tile_spec(batch_major_out, 0),
        scratch_shapes=scratch,
        compiler_params=pltpu.CompilerParams(
            dimension_semantics=("arbitrary",), vmem_limit_bytes=VMEM_LIMIT),
        name="ffn",
    )(*args)


def _normed_block(xp_ref, xm_ref, xn_ref, g_ref, blk, nblk):
    g = g_ref[...]
    hp = _rms(jnp.where(blk == 0, 0.0, xp_ref[...]), g).astype(BF16)
    hm = _rms(xm_ref[...], g).astype(BF16)
    hn = _rms(jnp.where(blk == nblk - 1, 0.0, xn_ref[...]), g).astype(BF16)
    return hm, jnp.concatenate([hp, hm, hn], axis=0)


def _quarter_decay(lam_ref):
    y = -lam_ref[...]
    softplus = jnp.maximum(y, 0.0) + jnp.log1p(jnp.exp(-jnp.abs(y)))
    return (0.25 * LRU_C) * softplus


def _lru_direction(xc_ref, w_ref, b_ref, qd, a_ref, u_ref, h_ref, o_ref, reverse):
    chunks = range(MIX_ROWS // GATE_CHUNK)
    steps = range(GATE_CHUNK // ROWS_PER_STEP)
    h_state = h_ref[...]
    for c in (reversed(chunks) if reverse else chunks):
        rs = slice(c * GATE_CHUNK, (c + 1) * GATE_CHUNK)
        for h in range(D_A // GATE_TILE):
            hs = slice(h * GATE_TILE, (h + 1) * GATE_TILE)
            xc = xc_ref[rs, hs]
            z = jnp.dot(xc.astype(BF16), w_ref[h], preferred_element_type=F32)
            t_r = jnp.tanh(z[:, :GATE_TILE] + b_ref[0:1, hs])
            t_i = jnp.tanh(z[:, GATE_TILE:] + b_ref[1:2, hs])
            n = jnp.tanh(t_r * qd[:, hs] + qd[:, hs])
            q = 1.0 / (1.0 + n)
            root = jnp.where(n == 0.0, 0.0, n * lax.rsqrt(n))
            a_ref[rs, hs] = (1.0 - n) * q
            u_ref[rs, hs] = (q * root) * ((t_i + 1.0) * xc)
        for s in (reversed(steps) if reverse else steps):
            ts = slice(c * GATE_CHUNK + s * ROWS_PER_STEP, c * GATE_CHUNK + (s + 1) * ROWS_PER_STEP)
            h_state = a_ref[ts, :] * h_state + u_ref[ts, :]
            o_ref[ts, :] = h_state
    h_ref[...] = h_state


def _mix_bwd_kernel(xp_ref, xm_ref, xn_ref, g_ref, wxa_ref, cw_ref, cb_ref, gw_ref, gb_ref,
                    lam_ref, xc_ref, hb_ref, xa_s, a_s, u_s, h_s, *, nblk):
    i = pl.program_id(0)
    blk = nblk - 1 - i

    @pl.when(i == 0)
    def _():
        h_s[...] = jnp.zeros_like(h_s)

    _, he = _normed_block(xp_ref, xm_ref, xn_ref, g_ref, blk, nblk)
    xa_s[...] = jnp.dot(he, wxa_ref[...], preferred_element_type=F32)
    xc = cb_ref[...] + cw_ref[0:1] * xa_s[0:MIX_ROWS]
    for k in range(1, 4):
        xc = xc + cw_ref[k:k + 1] * xa_s[k * ROWS_PER_STEP:k * ROWS_PER_STEP + MIX_ROWS]
    xc_ref[...] = xc
    _lru_direction(xc_ref, gw_ref, gb_ref, _quarter_decay(lam_ref), a_s, u_s, h_s, hb_ref, reverse=True)


def _halo_specs(n, tblk):
    per = MIX_ROWS // HALO
    last = n // HALO - 1
    prev = pl.BlockSpec((HALO, D_MODEL), lambda i: (jnp.maximum(tblk(i) * per - 1, 0), 0))
    main = pl.BlockSpec((MIX_ROWS, D_MODEL), lambda i: (tblk(i), 0))
    nxt = pl.BlockSpec((HALO, D_MODEL), lambda i: (jnp.minimum((tblk(i) + 1) * per, last), 0))
    return prev, main, nxt


def _mix_bwd(x, g, w_in, conv_w, conv_b, gate_w, gate_b, lam):
    n = x.shape[0]
    nblk = n // MIX_ROWS
    prev, main, nxt = _halo_specs(n, lambda i: nblk - 1 - i)
    part = pl.BlockSpec((MIX_ROWS, D_A), lambda i: (nblk - 1 - i, 0))
    return pl.pallas_call(
        functools.partial(_mix_bwd_kernel, nblk=nblk),
        out_shape=[jax.ShapeDtypeStruct((n, D_A), F32)] * 2,
        grid=(nblk,),
        in_specs=[prev, main, nxt, _const_spec((1, D_MODEL)),
                  _const_spec((D_MODEL, D_A)),
                  _const_spec((4, D_A)), _const_spec((1, D_A)),
                  _const_spec((D_A // GATE_TILE, GATE_TILE, 2 * GATE_TILE)),
                  _const_spec((2, D_A)), _const_spec((1, D_A))],
        out_specs=[part, part],
        scratch_shapes=[
            pltpu.VMEM((MIX_ROWS + 2 * HALO, D_A), F32),
            pltpu.VMEM((MIX_ROWS, D_A), F32),
            pltpu.VMEM((MIX_ROWS, D_A), F32),
            pltpu.VMEM((ROWS_PER_STEP, D_A), F32),
        ],
        compiler_params=pltpu.CompilerParams(
            dimension_semantics=("arbitrary",), vmem_limit_bytes=VMEM_LIMIT),
        name="mix_bwd",
    )(x, x, x, g, w_in, conv_w, conv_b, gate_w, gate_b, lam)


def _mix_fwd_kernel(xp_ref, xm_ref, xn_ref, g_ref, wga_ref, wbg_ref, wcg_ref, wxb_ref,
                    xc_ref, hb_ref, sw_ref, gw_ref, gb_ref, lam_ref, wo_ref,
                    o_ref, a_s, u_s, hf_s, gg_s, cx_s, yb_s, h_s, *, nblk):
    i = pl.program_id(0)

    @pl.when(i == 0)
    def _():
        h_s[...] = jnp.zeros_like(h_s)

    hm, he = _normed_block(xp_ref, xm_ref, xn_ref, g_ref, i, nblk)
    ext_rows = MIX_ROWS + 2 * HALO
    for r0 in range(0, MIX_ROWS, DOT_ROWS):
        rs = slice(r0, r0 + DOT_ROWS)
        gg_s[rs] = jax.nn.gelu(jnp.dot(hm[rs], wga_ref[...], preferred_element_type=F32))
    for r0 in range(0, ext_rows, ext_rows // 3):
        es = slice(r0, r0 + ext_rows // 3)
        cx_s[es] = (jnp.dot(he[es], wcg_ref[...], preferred_element_type=F32)
                    * jnp.dot(he[es], wxb_ref[...], preferred_element_type=F32))
    for r0 in range(0, MIX_ROWS, DOT_ROWS):
        rs = slice(r0, r0 + DOT_ROWS)
        lo = r0 + HALO - ROWS_PER_STEP
        conv = sw_ref[0:1] * cx_s[lo:lo + DOT_ROWS]
        for k in range(1, 3):
            conv = conv + sw_ref[k:k + 1] * cx_s[lo + k * ROWS_PER_STEP:lo + k * ROWS_PER_STEP + DOT_ROWS]
        yb_s[rs] = (jnp.dot(hm[rs], wbg_ref[...], preferred_element_type=F32) * conv).astype(BF16)

    _lru_direction(xc_ref, gw_ref, gb_ref, _quarter_decay(lam_ref), a_s, u_s, h_s, hf_s, reverse=False)

    for r0 in range(0, MIX_ROWS, DOT_ROWS):
        rs = slice(r0, r0 + DOT_ROWS)
        ya = ((hf_s[rs] + hb_ref[rs]) * gg_s[rs]).astype(BF16)
        y = jnp.dot(ya, wo_ref[0:D_A], preferred_element_type=F32)
        y = y + jnp.dot(yb_s[rs], wo_ref[D_A:], preferred_element_type=F32)
        o_ref[rs] = xm_ref[rs] + y


def _mix_fwd(x, xc, hb, g, w_in, sc_w, gate_w, gate_b, lam, w_out):
    n = x.shape[0]
    nblk = n // MIX_ROWS
    prev, main, nxt = _halo_specs(n, lambda i: i)
    part = pl.BlockSpec((MIX_ROWS, D_A), lambda i: (i, 0))
    w_cols = [_const_spec((D_MODEL, D_A), (0, k)) for k in range(1, 5)]
    return pl.pallas_call(
        functools.partial(_mix_fwd_kernel, nblk=nblk),
        out_shape=jax.ShapeDtypeStruct((n, D_MODEL), F32),
        grid=(nblk,),
        in_specs=[prev, main, nxt, _const_spec((1, D_MODEL)), *w_cols,
                  part, part, _const_spec((3, D_B)),
                  _const_spec((D_A // GATE_TILE, GATE_TILE, 2 * GATE_TILE)),
                  _const_spec((2, D_A)), _const_spec((1, D_A)),
                  _const_spec((D_MODEL, D_MODEL))],
        out_specs=main,
        scratch_shapes=[
            pltpu.VMEM((MIX_ROWS, D_A), F32),
            pltpu.VMEM((MIX_ROWS, D_A), F32),
            pltpu.VMEM((MIX_ROWS, D_A), F32),
            pltpu.VMEM((MIX_ROWS, D_A), F32),
            pltpu.VMEM((MIX_ROWS + 2 * HALO, D_B), F32),
            pltpu.VMEM((MIX_ROWS, D_B), BF16),
            pltpu.VMEM((ROWS_PER_STEP, D_A), F32),
        ],
        compiler_params=pltpu.CompilerParams(
            dimension_semantics=("arbitrary",), vmem_limit_bytes=VMEM_LIMIT),
        name="mix_fwd",
    )(x, x, x, g, w_in, w_in, w_in, w_in, xc, hb, sc_w, gate_w, gate_b, lam, w_out)


def _gate_tiles(w_r, w_i):
    per = GATE_TILE // LRU_BW
    eye = jnp.eye(per, dtype=w_r.dtype)

    def tiles(w):
        w = w.reshape(D_A // GATE_TILE, per, LRU_BW, LRU_BW)
        return jnp.einsum("hnkj,nm->hnkmj", w, eye).reshape(D_A // GATE_TILE, GATE_TILE, GATE_TILE)

    return (0.5 * jnp.concatenate([tiles(w_r), tiles(w_i)], axis=-1)).astype(BF16)


def _trunk(x, p):
    bsz, t, d = x.shape
    assert bsz == ROWS_PER_STEP and d == D_MODEL and (t * bsz) % MIX_ROWS == 0
    depth = p["w_in"].shape[0]
    for l in range(depth):
        x = _ffn(x, p["ffn1_norm"][l], p["ffn1_w_gate"][l], p["ffn1_w_up"][l], p["ffn1_w_down"][l],
                 batch_major_in=(l == 0))
        xc, hb = _mix_bwd(x, p["mix_norm"][l], p["w_in"][l], p["lru_conv_w"][l], p["lru_conv_b"][l],
                          p["gate_w"][l][1], p["gate_b"][l][1], p["lru_lambda"][l][1])
        x = _mix_fwd(x, xc, hb, p["mix_norm"][l], p["w_in"][l], p["sc_conv_w"][l],
                     p["gate_w"][l][0], p["gate_b"][l][0], p["lru_lambda"][l][0], p["w_out"][l])
        last = l == depth - 1
        x = _ffn(x, p["ffn2_norm"][l], p["ffn2_w_gate"][l], p["ffn2_w_up"][l], p["ffn2_w_down"][l],
                 final_g=p["final_norm"] if last else None, batch_major_out=last)
    return x


def kernel(x_prompt, x_sample, ffn1_norm, ffn1_w_gate, ffn1_w_up, ffn1_w_down, mix_norm, w_in, lru_conv_w, lru_conv_b, lru_w_r, lru_b_r, lru_w_i, lru_b_i, lru_lambda, sc_conv_w, w_out, ffn2_norm, ffn2_w_gate, ffn2_w_up, ffn2_w_down, final_norm):
    depth = w_in.shape[0]
    p = dict(
        ffn1_norm=ffn1_norm.reshape(depth, 1, D_MODEL),
        ffn1_w_gate=ffn1_w_gate.astype(BF16), ffn1_w_up=ffn1_w_up.astype(BF16),
        ffn1_w_down=ffn1_w_down.astype(BF16),
        mix_norm=mix_norm.reshape(depth, 1, D_MODEL),
        w_in=w_in.astype(BF16),
        lru_conv_w=lru_conv_w, lru_conv_b=lru_conv_b.reshape(depth, 1, D_A),
        gate_w=jax.vmap(jax.vmap(_gate_tiles))(lru_w_r, lru_w_i),
        gate_b=0.5 * jnp.stack([lru_b_r, lru_b_i], axis=2),
        lru_lambda=lru_lambda.reshape(depth, 2, 1, D_A),
        sc_conv_w=sc_conv_w, w_out=w_out.astype(BF16),
        ffn2_norm=ffn2_norm.reshape(depth, 1, D_MODEL),
        ffn2_w_gate=ffn2_w_gate.astype(BF16), ffn2_w_up=ffn2_w_up.astype(BF16),
        ffn2_w_down=ffn2_w_down.astype(BF16),
        final_norm=final_norm.reshape(1, D_MODEL),
    )
    return (_trunk(x_prompt, p), _trunk(x_sample, p))
```

```python
import functools

import jax
import jax.numpy as jnp
from jax import lax
from jax.experimental import pallas as pl
from jax.experimental.pallas import tpu as pltpu

D_MODEL = 1024
D_A = 512
D_B = 512
D_FF = 2816
P_IN = 2 * D_A + 3 * D_B
N_LRU_BLOCKS = 8
LRU_BW = 64
LRU_C = 8.0
EPS = 1e-6

LANES = 128
ROWS_PER_STEP = 8
HALO = 2 * ROWS_PER_STEP
GATE_TILE = 256

FFN_ROWS = 1024
FFN_CHUNK = 256
MIX_ROWS = 1024
MIX_BWD_ROWS = 2048
GATE_CHUNK = 256
DOT_ROWS = 512
VMEM_LIMIT = 56 * 1024 * 1024

F32 = jnp.float32
BF16 = jnp.bfloat16


def _rms(x, g):
    ms = jnp.mean(x * x, axis=-1, keepdims=True)
    return x * lax.rsqrt(ms + EPS) * g


def _const_spec(shape, index=None, lead=()):
    index = tuple(lead) + ((0,) * len(shape) if index is None else tuple(index))
    return pl.BlockSpec((None,) * len(lead) + tuple(shape), lambda i: index, pipeline_mode=pl.Buffered(1))


def _to_time_major(x_ref, xs_ref):
    steps = FFN_ROWS // ROWS_PER_STEP
    for b in range(ROWS_PER_STEP):
        for j in range(D_MODEL // LANES):
            xs_ref[j, pl.ds(b, steps, stride=ROWS_PER_STEP), :] = x_ref[b, :, j * LANES:(j + 1) * LANES]
    return jnp.concatenate([xs_ref[j] for j in range(D_MODEL // LANES)], axis=1)


def _ffn_kernel(*refs, final, batch_major_in, batch_major_out):
    x_ref, xnext_ref, g_ref, wg_ref, wu_ref, wd_ref = refs[:6]
    refs = refs[6:]
    if final:
        fg_ref, refs = refs[0], refs[1:]
    o_ref, act_ref, hb_ref, act0_ref = refs[:4]
    refs = refs[4:]
    if batch_major_in:
        xs_ref, refs = refs[0], refs[1:]
        load = functools.partial(_to_time_major, xs_ref=xs_ref)
    else:
        load = lambda ref: ref[...]

    def hidden_chunk(c):
        cs = slice(c * FFN_CHUNK, (c + 1) * FFN_CHUNK)
        g = jnp.dot(hb_ref[...], wg_ref[:, cs], preferred_element_type=F32)
        u = jnp.dot(hb_ref[...], wu_ref[:, cs], preferred_element_type=F32)
        return ((g * (0.5 * jnp.tanh(0.5 * g) + 0.5)) * u).astype(BF16)

    def start_tile(ref):
        hb_ref[...] = _rms(load(ref), g_ref[...]).astype(BF16)
        act0_ref[...] = hidden_chunk(0)

    @pl.when(pl.program_id(0) == 0)
    def _():
        start_tile(x_ref)

    act_ref[:, 0:FFN_CHUNK] = act0_ref[...]
    for c in range(1, D_FF // FFN_CHUNK):
        act_ref[:, c * FFN_CHUNK:(c + 1) * FFN_CHUNK] = hidden_chunk(c)
    start_tile(xnext_ref)
    y = jnp.dot(act_ref[...], wd_ref[...], preferred_element_type=F32)
    out = load(x_ref) + 0.5 * y
    if final:
        out = _rms(out, fg_ref[...])
    if batch_major_out:
        os_ref = refs[0]
        steps = FFN_ROWS // ROWS_PER_STEP
        for j in range(D_MODEL // LANES):
            os_ref[j] = out[:, j * LANES:(j + 1) * LANES]
        for b in range(ROWS_PER_STEP):
            for j in range(D_MODEL // LANES):
                o_ref[b, :, j * LANES:(j + 1) * LANES] = os_ref[j, pl.ds(b, steps, stride=ROWS_PER_STEP), :]
    else:
        o_ref[...] = out


def _ffn(x, layer, g, wg, wu, wd, final_g=None, batch_major_in=False, batch_major_out=False):
    steps = FFN_ROWS // ROWS_PER_STEP
    if batch_major_in:
        bsz, t, _ = x.shape
        n = bsz * t
    else:
        n = x.shape[0]
        bsz, t = ROWS_PER_STEP, n // ROWS_PER_STEP
    nblk = n // FFN_ROWS
    final = final_g is not None

    def tile_spec(batch_major, ahead):
        if batch_major:
            return pl.BlockSpec((bsz, steps, D_MODEL), lambda i: (0, jnp.minimum(i + ahead, nblk - 1), 0))
        return pl.BlockSpec((FFN_ROWS, D_MODEL), lambda i: (jnp.minimum(i + ahead, nblk - 1), 0))

    in_specs = [tile_spec(batch_major_in, 0), tile_spec(batch_major_in, 1),
                _const_spec((1, D_MODEL), lead=(layer,)), _const_spec((D_MODEL, D_FF), lead=(layer,)),
                _const_spec((D_MODEL, D_FF), lead=(layer,)), _const_spec((D_FF, D_MODEL), lead=(layer,))]
    args = [x, x, g, wg, wu, wd]
    if final:
        in_specs.append(_const_spec((1, D_MODEL)))
        args.append(final_g)
    scratch = [pltpu.VMEM((FFN_ROWS, D_FF), BF16), pltpu.VMEM((FFN_ROWS, D_MODEL), BF16),
               pltpu.VMEM((FFN_ROWS, FFN_CHUNK), BF16)]
    scratch += ([pltpu.VMEM((D_MODEL // LANES, FFN_ROWS, LANES), F32)]
                * (int(batch_major_in) + int(batch_major_out)))
    out_shape = (bsz, t, D_MODEL) if batch_major_out else (n, D_MODEL)
    return pl.pallas_call(
        functools.partial(_ffn_kernel, final=final, batch_major_in=batch_major_in,
                          batch_major_out=batch_major_out),
        out_shape=jax.ShapeDtypeStruct(out_shape, F32),
        grid=(nblk,),
        in_specs=in_specs,
        out_specs=tile_spec(batch_major_out, 0),
        scratch_shapes=scratch,
        compiler_params=pltpu.CompilerParams(
            dimension_semantics=("arbitrary",), vmem_limit_bytes=VMEM_LIMIT),
        name="ffn",
    )(*args)


def _normed_block(xp_ref, xm_ref, xn_ref, g_ref, blk, nblk):
    g = g_ref[...]
    hp = _rms(jnp.where(blk == 0, 0.0, xp_ref[...]), g).astype(BF16)
    hm = _rms(xm_ref[...], g).astype(BF16)
    hn = _rms(jnp.where(blk == nblk - 1, 0.0, xn_ref[...]), g).astype(BF16)
    return hm, jnp.concatenate([hp, hm, hn], axis=0)


def _quarter_decay(lam_ref):
    y = -lam_ref[...]
    softplus = jnp.maximum(y, 0.0) + jnp.log1p(jnp.exp(-jnp.abs(y)))
    return (0.25 * LRU_C) * softplus


def _lru_direction(xc_ref, w_ref, b_ref, qd, a_ref, u_ref, h_ref, o_ref, reverse):
    chunks = range(xc_ref.shape[0] // GATE_CHUNK)
    steps = range(GATE_CHUNK // ROWS_PER_STEP)
    h_state = h_ref[...]
    for c in (reversed(chunks) if reverse else chunks):
        rs = slice(c * GATE_CHUNK, (c + 1) * GATE_CHUNK)
        for h in range(D_A // GATE_TILE):
            hs = slice(h * GATE_TILE, (h + 1) * GATE_TILE)
            xc = xc_ref[rs, hs]
            z = jnp.dot(xc.astype(BF16), w_ref[h], preferred_element_type=F32)
            t_r = jnp.tanh(z[:, :GATE_TILE] + b_ref[0:1, hs])
            t_i = jnp.tanh(z[:, GATE_TILE:] + b_ref[1:2, hs])
            n = jnp.tanh(t_r * qd[:, hs] + qd[:, hs])
            q = 1.0 / (1.0 + n)
            root = jnp.where(n == 0.0, 0.0, n * lax.rsqrt(n))
            a_ref[rs, hs] = (1.0 - n) * q
            u_ref[rs, hs] = (q * root) * ((t_i + 1.0) * xc)
        for s in (reversed(steps) if reverse else steps):
            ts = slice(c * GATE_CHUNK + s * ROWS_PER_STEP, c * GATE_CHUNK + (s + 1) * ROWS_PER_STEP)
            h_state = a_ref[ts, :] * h_state + u_ref[ts, :]
            o_ref[ts, :] = h_state
    h_ref[...] = h_state


def _mix_bwd_kernel(xp_ref, xm_ref, xn_ref, g_ref, wxa_ref, cw_ref, cb_ref, gw_ref, gb_ref,
                    lam_ref, xc_ref, hb_ref, xa_s, a_s, u_s, h_s, *, nblk):
    i = pl.program_id(0)
    blk = nblk - 1 - i
    rows = xm_ref.shape[0]

    @pl.when(i == 0)
    def _():
        h_s[...] = jnp.zeros_like(h_s)

    _, he = _normed_block(xp_ref, xm_ref, xn_ref, g_ref, blk, nblk)
    xa_s[...] = jnp.dot(he, wxa_ref[...], preferred_element_type=F32)
    xc = cb_ref[...] + cw_ref[0:1] * xa_s[0:rows]
    for k in range(1, 4):
        xc = xc + cw_ref[k:k + 1] * xa_s[k * ROWS_PER_STEP:k * ROWS_PER_STEP + rows]
    xc_ref[...] = xc
    _lru_direction(xc_ref, gw_ref, gb_ref, _quarter_decay(lam_ref), a_s, u_s, h_s, hb_ref, reverse=True)


def _halo_specs(n, rows, tblk):
    per = rows // HALO
    last = n // HALO - 1
    prev = pl.BlockSpec((HALO, D_MODEL), lambda i: (jnp.maximum(tblk(i) * per - 1, 0), 0))
    main = pl.BlockSpec((rows, D_MODEL), lambda i: (tblk(i), 0))
    nxt = pl.BlockSpec((HALO, D_MODEL), lambda i: (jnp.minimum((tblk(i) + 1) * per, last), 0))
    return prev, main, nxt


def _mix_bwd(x, layer, g, w_in, conv_w, conv_b, gate_w, gate_b, lam):
    n = x.shape[0]
    rows = MIX_BWD_ROWS
    nblk = n // rows
    prev, main, nxt = _halo_specs(n, rows, lambda i: nblk - 1 - i)
    part = pl.BlockSpec((rows, D_A), lambda i: (nblk - 1 - i, 0))
    return pl.pallas_call(
        functools.partial(_mix_bwd_kernel, nblk=nblk),
        out_shape=[jax.ShapeDtypeStruct((n, D_A), F32)] * 2,
        grid=(nblk,),
        in_specs=[prev, main, nxt, _const_spec((1, D_MODEL), lead=(layer,)),
                  _const_spec((D_MODEL, D_A), lead=(layer,)),
                  _const_spec((4, D_A), lead=(layer,)), _const_spec((1, D_A), lead=(layer,)),
                  _const_spec((D_A // GATE_TILE, GATE_TILE, 2 * GATE_TILE), lead=(layer, 1)),
                  _const_spec((2, D_A), lead=(layer, 1)), _const_spec((1, D_A), lead=(layer, 1))],
        out_specs=[part, part],
        scratch_shapes=[
            pltpu.VMEM((rows + 2 * HALO, D_A), F32),
            pltpu.VMEM((rows, D_A), F32),
            pltpu.VMEM((rows, D_A), F32),
            pltpu.VMEM((ROWS_PER_STEP, D_A), F32),
        ],
        compiler_params=pltpu.CompilerParams(
            dimension_semantics=("arbitrary",), vmem_limit_bytes=VMEM_LIMIT),
        name="mix_bwd",
    )(x, x, x, g, w_in, conv_w, conv_b, gate_w, gate_b, lam)


def _mix_fwd_kernel(xp_ref, xm_ref, xn_ref, g_ref, wga_ref, wbg_ref, wcg_ref, wxb_ref,
                    xc_ref, hb_ref, sw_ref, gw_ref, gb_ref, lam_ref, wo_ref,
                    o_ref, a_s, u_s, hf_s, gg_s, cx_s, yb_s, h_s, *, nblk):
    i = pl.program_id(0)

    @pl.when(i == 0)
    def _():
        h_s[...] = jnp.zeros_like(h_s)

    hm, he = _normed_block(xp_ref, xm_ref, xn_ref, g_ref, i, nblk)
    ext_rows = MIX_ROWS + 2 * HALO
    for r0 in range(0, MIX_ROWS, DOT_ROWS):
        rs = slice(r0, r0 + DOT_ROWS)
        gg_s[rs] = jax.nn.gelu(jnp.dot(hm[rs], wga_ref[...], preferred_element_type=F32))
    for r0 in range(0, ext_rows, ext_rows // 3):
        es = slice(r0, r0 + ext_rows // 3)
        cx_s[es] = (jnp.dot(he[es], wcg_ref[...], preferred_element_type=F32)
                    * jnp.dot(he[es], wxb_ref[...], preferred_element_type=F32))
    for r0 in range(0, MIX_ROWS, DOT_ROWS):
        rs = slice(r0, r0 + DOT_ROWS)
        lo = r0 + HALO - ROWS_PER_STEP
        conv = sw_ref[0:1] * cx_s[lo:lo + DOT_ROWS]
        for k in range(1, 3):
            conv = conv + sw_ref[k:k + 1] * cx_s[lo + k * ROWS_PER_STEP:lo + k * ROWS_PER_STEP + DOT_ROWS]
        yb_s[rs] = (jnp.dot(hm[rs], wbg_ref[...], preferred_element_type=F32) * conv).astype(BF16)

    _lru_direction(xc_ref, gw_ref, gb_ref, _quarter_decay(lam_ref), a_s, u_s, h_s, hf_s, reverse=False)

    for r0 in range(0, MIX_ROWS, DOT_ROWS):
        rs = slice(r0, r0 + DOT_ROWS)
        ya = ((hf_s[rs] + hb_ref[rs]) * gg_s[rs]).astype(BF16)
        y = jnp.dot(ya, wo_ref[0:D_A], preferred_element_type=F32)
        y = y + jnp.dot(yb_s[rs], wo_ref[D_A:], preferred_element_type=F32)
        o_ref[rs] = xm_ref[rs] + y


def _mix_fwd(x, xc, hb, layer, g, w_in, sc_w, gate_w, gate_b, lam, w_out):
    n = x.shape[0]
    nblk = n // MIX_ROWS
    prev, main, nxt = _halo_specs(n, MIX_ROWS, lambda i: i)
    part = pl.BlockSpec((MIX_ROWS, D_A), lambda i: (i, 0))
    w_cols = [_const_spec((D_MODEL, D_A), (0, k), lead=(layer,)) for k in range(1, 5)]
    return pl.pallas_call(
        functools.partial(_mix_fwd_kernel, nblk=nblk),
        out_shape=jax.ShapeDtypeStruct((n, D_MODEL), F32),
        grid=(nblk,),
        in_specs=[prev, main, nxt, _const_spec((1, D_MODEL), lead=(layer,)), *w_cols,
                  part, part, _const_spec((3, D_B), lead=(layer,)),
                  _const_spec((D_A // GATE_TILE, GATE_TILE, 2 * GATE_TILE), lead=(layer, 0)),
                  _const_spec((2, D_A), lead=(layer, 0)), _const_spec((1, D_A), lead=(layer, 0)),
                  _const_spec((D_MODEL, D_MODEL), lead=(layer,))],
        out_specs=main,
        scratch_shapes=[
            pltpu.VMEM((MIX_ROWS, D_A), F32),
            pltpu.VMEM((MIX_ROWS, D_A), F32),
            pltpu.VMEM((MIX_ROWS, D_A), F32),
            pltpu.VMEM((MIX_ROWS, D_A), F32),
            pltpu.VMEM((MIX_ROWS + 2 * HALO, D_B), F32),
            pltpu.VMEM((MIX_ROWS, D_B), BF16),
            pltpu.VMEM((ROWS_PER_STEP, D_A), F32),
        ],
        compiler_params=pltpu.CompilerParams(
            dimension_semantics=("arbitrary",), vmem_limit_bytes=VMEM_LIMIT),
        name="mix_fwd",
    )(x, x, x, g, w_in, w_in, w_in, w_in, xc, hb, sc_w, gate_w, gate_b, lam, w_out)


def _gate_tiles(w_r, w_i):
    per = GATE_TILE // LRU_BW
    eye = jnp.eye(per, dtype=w_r.dtype)

    def tiles(w):
        w = w.reshape(D_A // GATE_TILE, per, LRU_BW, LRU_BW)
        return jnp.einsum("hnkj,nm->hnkmj", w, eye).reshape(D_A // GATE_TILE, GATE_TILE, GATE_TILE)

    return (0.5 * jnp.concatenate([tiles(w_r), tiles(w_i)], axis=-1)).astype(BF16)


def _trunk(x, p):
    bsz, t, d = x.shape
    assert bsz == ROWS_PER_STEP and d == D_MODEL and (t * bsz) % max(MIX_ROWS, MIX_BWD_ROWS) == 0
    depth = p["w_in"].shape[0]
    for l in range(depth):
        x = _ffn(x, l, p["ffn1_norm"], p["ffn1_w_gate"], p["ffn1_w_up"], p["ffn1_w_down"],
                 batch_major_in=(l == 0))
        xc, hb = _mix_bwd(x, l, p["mix_norm"], p["w_in"], p["lru_conv_w"], p["lru_conv_b"],
                          p["gate_w"], p["gate_b"], p["lru_lambda"])
        x = _mix_fwd(x, xc, hb, l, p["mix_norm"], p["w_in"], p["sc_conv_w"],
                     p["gate_w"], p["gate_b"], p["lru_lambda"], p["w_out"])
        last = l == depth - 1
        x = _ffn(x, l, p["ffn2_norm"], p["ffn2_w_gate"], p["ffn2_w_up"], p["ffn2_w_down"],
                 final_g=p["final_norm"] if last else None, batch_major_out=last)
    return x


def kernel(x_prompt, x_sample, ffn1_norm, ffn1_w_gate, ffn1_w_up, ffn1_w_down, mix_norm, w_in, lru_conv_w, lru_conv_b, lru_w_r, lru_b_r, lru_w_i, lru_b_i, lru_lambda, sc_conv_w, w_out, ffn2_norm, ffn2_w_gate, ffn2_w_up, ffn2_w_down, final_norm):
    depth = w_in.shape[0]
    p = dict(
        ffn1_norm=ffn1_norm.reshape(depth, 1, D_MODEL),
        ffn1_w_gate=ffn1_w_gate.astype(BF16), ffn1_w_up=ffn1_w_up.astype(BF16),
        ffn1_w_down=ffn1_w_down.astype(BF16),
        mix_norm=mix_norm.reshape(depth, 1, D_MODEL),
        w_in=w_in.astype(BF16),
        lru_conv_w=lru_conv_w, lru_conv_b=lru_conv_b.reshape(depth, 1, D_A),
        gate_w=jax.vmap(jax.vmap(_gate_tiles))(lru_w_r, lru_w_i),
        gate_b=0.5 * jnp.stack([lru_b_r, lru_b_i], axis=2),
        lru_lambda=lru_lambda.reshape(depth, 2, 1, D_A),
        sc_conv_w=sc_conv_w, w_out=w_out.astype(BF16),
        ffn2_norm=ffn2_norm.reshape(depth, 1, D_MODEL),
        ffn2_w_gate=ffn2_w_gate.astype(BF16), ffn2_w_up=ffn2_w_up.astype(BF16),
        ffn2_w_down=ffn2_w_down.astype(BF16),
        final_norm=final_norm.reshape(1, D_MODEL),
    )
    return (_trunk(x_prompt, p), _trunk(x_sample, p))
```

```python
import functools

import jax
import jax.numpy as jnp
from jax import lax
from jax.experimental import pallas as pl
from jax.experimental.pallas import tpu as pltpu

D_MODEL = 1024
D_A = 512
D_B = 512
D_FF = 2816
P_IN = 2 * D_A + 3 * D_B
N_LRU_BLOCKS = 8
LRU_BW = 64
LRU_C = 8.0
EPS = 1e-6

LANES = 128
ROWS_PER_STEP = 8
HALO = 2 * ROWS_PER_STEP
GATE_TILE = 256

FFN_ROWS = 1024
FFN_CHUNK = 256
MIX_ROWS = 1024
MIX_BWD_ROWS = 2048
GATE_CHUNK = 256
DOT_ROWS = 512
VMEM_LIMIT = 56 * 1024 * 1024

F32 = jnp.float32
BF16 = jnp.bfloat16


def _rms(x, g):
    ms = jnp.mean(x * x, axis=-1, keepdims=True)
    return x * lax.rsqrt(ms + EPS) * g


def _const_spec(shape, index=None, lead=()):
    index = tuple(lead) + ((0,) * len(shape) if index is None else tuple(index))
    return pl.BlockSpec((None,) * len(lead) + tuple(shape), lambda i: index, pipeline_mode=pl.Buffered(1))


def _to_time_major(x_ref, xs_ref):
    steps = FFN_ROWS // ROWS_PER_STEP
    for b in range(ROWS_PER_STEP):
        for j in range(D_MODEL // LANES):
            xs_ref[j, pl.ds(b, steps, stride=ROWS_PER_STEP), :] = x_ref[b, :, j * LANES:(j + 1) * LANES]
    return jnp.concatenate([xs_ref[j] for j in range(D_MODEL // LANES)], axis=1)


def _ffn_kernel(*refs, final, batch_major_in, batch_major_out):
    x_ref, g_ref, wg_ref, wu_ref, wd_ref = refs[:5]
    refs = refs[5:]
    if final:
        fg_ref, refs = refs[0], refs[1:]
    o_ref, act_ref, hb_ref, act0_ref, xres_ref = refs[:5]
    refs = refs[5:]
    if batch_major_in:
        xs_ref, refs = refs[0], refs[1:]
    i = pl.program_id(0)

    def hidden_chunk(c):
        cs = slice(c * FFN_CHUNK, (c + 1) * FFN_CHUNK)
        g = jnp.dot(hb_ref[...], wg_ref[:, cs], preferred_element_type=F32)
        u = jnp.dot(hb_ref[...], wu_ref[:, cs], preferred_element_type=F32)
        return ((g * (0.5 * jnp.tanh(0.5 * g) + 0.5)) * u).astype(BF16)

    def start_tile():
        x = _to_time_major(x_ref, xs_ref) if batch_major_in else x_ref[...]
        xres_ref[i % 2] = x
        hb_ref[...] = _rms(x, g_ref[...]).astype(BF16)
        act0_ref[...] = hidden_chunk(0)

    @pl.when(i == 0)
    def _():
        start_tile()

    @pl.when(i > 0)
    def _():
        act_ref[:, 0:FFN_CHUNK] = act0_ref[...]
        for c in range(1, D_FF // FFN_CHUNK):
            act_ref[:, c * FFN_CHUNK:(c + 1) * FFN_CHUNK] = hidden_chunk(c)
        start_tile()
        y = jnp.dot(act_ref[...], wd_ref[...], preferred_element_type=F32)
        out = xres_ref[(i + 1) % 2] + 0.5 * y
        if final:
            out = _rms(out, fg_ref[...])
        if batch_major_out:
            os_ref = refs[0]
            steps = FFN_ROWS // ROWS_PER_STEP
            for j in range(D_MODEL // LANES):
                os_ref[j] = out[:, j * LANES:(j + 1) * LANES]
            for b in range(ROWS_PER_STEP):
                for j in range(D_MODEL // LANES):
                    o_ref[b, :, j * LANES:(j + 1) * LANES] = os_ref[j, pl.ds(b, steps, stride=ROWS_PER_STEP), :]
        else:
            o_ref[...] = out


def _ffn(x, layer, g, wg, wu, wd, final_g=None, batch_major_in=False, batch_major_out=False):
    steps = FFN_ROWS // ROWS_PER_STEP
    if batch_major_in:
        bsz, t, _ = x.shape
        n = bsz * t
    else:
        n = x.shape[0]
        bsz, t = ROWS_PER_STEP, n // ROWS_PER_STEP
    nblk = n // FFN_ROWS
    final = final_g is not None

    def tile_spec(batch_major, tile):
        if batch_major:
            return pl.BlockSpec((bsz, steps, D_MODEL), lambda i: (0, tile(i), 0))
        return pl.BlockSpec((FFN_ROWS, D_MODEL), lambda i: (tile(i), 0))

    in_specs = [tile_spec(batch_major_in, lambda i: jnp.minimum(i, nblk - 1)),
                _const_spec((1, D_MODEL), lead=(layer,)), _const_spec((D_MODEL, D_FF), lead=(layer,)),
                _const_spec((D_MODEL, D_FF), lead=(layer,)), _const_spec((D_FF, D_MODEL), lead=(layer,))]
    args = [x, g, wg, wu, wd]
    if final:
        in_specs.append(_const_spec((1, D_MODEL)))
        args.append(final_g)
    scratch = [pltpu.VMEM((FFN_ROWS, D_FF), BF16), pltpu.VMEM((FFN_ROWS, D_MODEL), BF16),
               pltpu.VMEM((FFN_ROWS, FFN_CHUNK), BF16), pltpu.VMEM((2, FFN_ROWS, D_MODEL), F32)]
    scratch += ([pltpu.VMEM((D_MODEL // LANES, FFN_ROWS, LANES), F32)]
                * (int(batch_major_in) + int(batch_major_out)))
    out_shape = (bsz, t, D_MODEL) if batch_major_out else (n, D_MODEL)
    return pl.pallas_call(
        functools.partial(_ffn_kernel, final=final, batch_major_in=batch_major_in,
                          batch_major_out=batch_major_out),
        out_shape=jax.ShapeDtypeStruct(out_shape, F32),
        grid=(nblk + 1,),
        in_specs=in_specs,
        out_specs=tile_spec(batch_major_out, lambda i: jnp.maximum(i - 1, 0)),
        scratch_shapes=scratch,
        compiler_params=pltpu.CompilerParams(
            dimension_semantics=("arbitrary",), vmem_limit_bytes=VMEM_LIMIT),
        name="ffn",
    )(*args)


def _normed_block(xp_ref, xm_ref, xn_ref, g_ref, blk, nblk):
    g = g_ref[...]
    hp = _rms(jnp.where(blk == 0, 0.0, xp_ref[...]), g).astype(BF16)
    hm = _rms(xm_ref[...], g).astype(BF16)
    hn = _rms(jnp.where(blk == nblk - 1, 0.0, xn_ref[...]), g).astype(BF16)
    return hm, jnp.concatenate([hp, hm, hn], axis=0)


def _quarter_decay(lam_ref):
    y = -lam_ref[...]
    softplus = jnp.maximum(y, 0.0) + jnp.log1p(jnp.exp(-jnp.abs(y)))
    return (0.25 * LRU_C) * softplus


def _lru_direction(xc_ref, w_ref, b_ref, qd, a_ref, u_ref, h_ref, o_ref, reverse):
    chunks = range(xc_ref.shape[0] // GATE_CHUNK)
    steps = range(GATE_CHUNK // ROWS_PER_STEP)
    h_state = h_ref[...]
    for c in (reversed(chunks) if reverse else chunks):
        rs = slice(c * GATE_CHUNK, (c + 1) * GATE_CHUNK)
        for h in range(D_A // GATE_TILE):
            hs = slice(h * GATE_TILE, (h + 1) * GATE_TILE)
            xc = xc_ref[rs, hs]
            z = jnp.dot(xc.astype(BF16), w_ref[h], preferred_element_type=F32)
            t_r = jnp.tanh(z[:, :GATE_TILE] + b_ref[0:1, hs])
            t_i = jnp.tanh(z[:, GATE_TILE:] + b_ref[1:2, hs])
            n = jnp.tanh(t_r * qd[:, hs] + qd[:, hs])
            q = 1.0 / (1.0 + n)
            root = jnp.where(n == 0.0, 0.0, n * lax.rsqrt(n))
            a_ref[rs, hs] = (1.0 - n) * q
            u_ref[rs, hs] = (q * root) * ((t_i + 1.0) * xc)
        for s in (reversed(steps) if reverse else steps):
            ts = slice(c * GATE_CHUNK + s * ROWS_PER_STEP, c * GATE_CHUNK + (s + 1) * ROWS_PER_STEP)
            h_state = a_ref[ts, :] * h_state + u_ref[ts, :]
            o_ref[ts, :] = h_state
    h_ref[...] = h_state


def _mix_bwd_kernel(xp_ref, xm_ref, xn_ref, g_ref, wxa_ref, cw_ref, cb_ref, gw_ref, gb_ref,
                    lam_ref, xc_ref, hb_ref, xa_s, a_s, u_s, h_s, *, nblk):
    i = pl.program_id(0)
    blk = nblk - 1 - i
    rows = xm_ref.shape[0]

    @pl.when(i == 0)
    def _():
        h_s[...] = jnp.zeros_like(h_s)

    _, he = _normed_block(xp_ref, xm_ref, xn_ref, g_ref, blk, nblk)
    xa_s[...] = jnp.dot(he, wxa_ref[...], preferred_element_type=F32)
    xc = cb_ref[...] + cw_ref[0:1] * xa_s[0:rows]
    for k in range(1, 4):
        xc = xc + cw_ref[k:k + 1] * xa_s[k * ROWS_PER_STEP:k * ROWS_PER_STEP + rows]
    xc_ref[...] = xc
    _lru_direction(xc_ref, gw_ref, gb_ref, _quarter_decay(lam_ref), a_s, u_s, h_s, hb_ref, reverse=True)


def _halo_specs(n, rows, tblk):
    per = rows // HALO
    last = n // HALO - 1
    prev = pl.BlockSpec((HALO, D_MODEL), lambda i: (jnp.maximum(tblk(i) * per - 1, 0), 0))
    main = pl.BlockSpec((rows, D_MODEL), lambda i: (tblk(i), 0))
    nxt = pl.BlockSpec((HALO, D_MODEL), lambda i: (jnp.minimum((tblk(i) + 1) * per, last), 0))
    return prev, main, nxt


def _mix_bwd(x, layer, g, w_in, conv_w, conv_b, gate_w, gate_b, lam):
    n = x.shape[0]
    rows = MIX_BWD_ROWS
    nblk = n // rows
    prev, main, nxt = _halo_specs(n, rows, lambda i: nblk - 1 - i)
    part = pl.BlockSpec((rows, D_A), lambda i: (nblk - 1 - i, 0))
    return pl.pallas_call(
        functools.partial(_mix_bwd_kernel, nblk=nblk),
        out_shape=[jax.ShapeDtypeStruct((n, D_A), F32)] * 2,
        grid=(nblk,),
        in_specs=[prev, main, nxt, _const_spec((1, D_MODEL), lead=(layer,)),
                  _const_spec((D_MODEL, D_A), lead=(layer,)),
                  _const_spec((4, D_A), lead=(layer,)), _const_spec((1, D_A), lead=(layer,)),
                  _const_spec((D_A // GATE_TILE, GATE_TILE, 2 * GATE_TILE), lead=(layer, 1)),
                  _const_spec((2, D_A), lead=(layer, 1)), _const_spec((1, D_A), lead=(layer, 1))],
        out_specs=[part, part],
        scratch_shapes=[
            pltpu.VMEM((rows + 2 * HALO, D_A), F32),
            pltpu.VMEM((rows, D_A), F32),
            pltpu.VMEM((rows, D_A), F32),
            pltpu.VMEM((ROWS_PER_STEP, D_A), F32),
        ],
        compiler_params=pltpu.CompilerParams(
            dimension_semantics=("arbitrary",), vmem_limit_bytes=VMEM_LIMIT),
        name="mix_bwd",
    )(x, x, x, g, w_in, conv_w, conv_b, gate_w, gate_b, lam)


def _mix_fwd_kernel(xp_ref, xm_ref, xn_ref, g_ref, wga_ref, wbg_ref, wcg_ref, wxb_ref,
                    xc_ref, hb_ref, sw_ref, gw_ref, gb_ref, lam_ref, wo_ref,
                    o_ref, a_s, u_s, hf_s, gg_s, cx_s, yb_s, h_s, *, nblk):
    i = pl.program_id(0)

    @pl.when(i == 0)
    def _():
        h_s[...] = jnp.zeros_like(h_s)

    hm, he = _normed_block(xp_ref, xm_ref, xn_ref, g_ref, i, nblk)
    ext_rows = MIX_ROWS + 2 * HALO
    for r0 in range(0, MIX_ROWS, DOT_ROWS):
        rs = slice(r0, r0 + DOT_ROWS)
        gg_s[rs] = jax.nn.gelu(jnp.dot(hm[rs], wga_ref[...], preferred_element_type=F32))
    for r0 in range(0, ext_rows, ext_rows // 3):
        es = slice(r0, r0 + ext_rows // 3)
        cx_s[es] = (jnp.dot(he[es], wcg_ref[...], preferred_element_type=F32)
                    * jnp.dot(he[es], wxb_ref[...], preferred_element_type=F32))
    for r0 in range(0, MIX_ROWS, DOT_ROWS):
        rs = slice(r0, r0 + DOT_ROWS)
        lo = r0 + HALO - ROWS_PER_STEP
        conv = sw_ref[0:1] * cx_s[lo:lo + DOT_ROWS]
        for k in range(1, 3):
            conv = conv + sw_ref[k:k + 1] * cx_s[lo + k * ROWS_PER_STEP:lo + k * ROWS_PER_STEP + DOT_ROWS]
        yb_s[rs] = (jnp.dot(hm[rs], wbg_ref[...], preferred_element_type=F32) * conv).astype(BF16)

    _lru_direction(xc_ref, gw_ref, gb_ref, _quarter_decay(lam_ref), a_s, u_s, h_s, hf_s, reverse=False)

    for r0 in range(0, MIX_ROWS, DOT_ROWS):
        rs = slice(r0, r0 + DOT_ROWS)
        ya = ((hf_s[rs] + hb_ref[rs]) * gg_s[rs]).astype(BF16)
        y = jnp.dot(ya, wo_ref[0:D_A], preferred_element_type=F32)
        y = y + jnp.dot(yb_s[rs], wo_ref[D_A:], preferred_element_type=F32)
        o_ref[rs] = xm_ref[rs] + y


def _mix_fwd(x, xc, hb, layer, g, w_in, sc_w, gate_w, gate_b, lam, w_out):
    n = x.shape[0]
    nblk = n // MIX_ROWS
    prev, main, nxt = _halo_specs(n, MIX_ROWS, lambda i: i)
    part = pl.BlockSpec((MIX_ROWS, D_A), lambda i: (i, 0))
    w_cols = [_const_spec((D_MODEL, D_A), (0, k), lead=(layer,)) for k in range(1, 5)]
    return pl.pallas_call(
        functools.partial(_mix_fwd_kernel, nblk=nblk),
        out_shape=jax.ShapeDtypeStruct((n, D_MODEL), F32),
        grid=(nblk,),
        in_specs=[prev, main, nxt, _const_spec((1, D_MODEL), lead=(layer,)), *w_cols,
                  part, part, _const_spec((3, D_B), lead=(layer,)),
                  _const_spec((D_A // GATE_TILE, GATE_TILE, 2 * GATE_TILE), lead=(layer, 0)),
                  _const_spec((2, D_A), lead=(layer, 0)), _const_spec((1, D_A), lead=(layer, 0)),
                  _const_spec((D_MODEL, D_MODEL), lead=(layer,))],
        out_specs=main,
        scratch_shapes=[
            pltpu.VMEM((MIX_ROWS, D_A), F32),
            pltpu.VMEM((MIX_ROWS, D_A), F32),
            pltpu.VMEM((MIX_ROWS, D_A), F32),
            pltpu.VMEM((MIX_ROWS, D_A), F32),
            pltpu.VMEM((MIX_ROWS + 2 * HALO, D_B), F32),
            pltpu.VMEM((MIX_ROWS, D_B), BF16),
            pltpu.VMEM((ROWS_PER_STEP, D_A), F32),
        ],
        compiler_params=pltpu.CompilerParams(
            dimension_semantics=("arbitrary",), vmem_limit_bytes=VMEM_LIMIT),
        name="mix_fwd",
    )(x, x, x, g, w_in, w_in, w_in, w_in, xc, hb, sc_w, gate_w, gate_b, lam, w_out)


def _gate_tiles(w_r, w_i):
    per = GATE_TILE // LRU_BW
    eye = jnp.eye(per, dtype=w_r.dtype)

    def tiles(w):
        w = w.reshape(D_A // GATE_TILE, per, LRU_BW, LRU_BW)
        return jnp.einsum("hnkj,nm->hnkmj", w, eye).reshape(D_A // GATE_TILE, GATE_TILE, GATE_TILE)

    return (0.5 * jnp.concatenate([tiles(w_r), tiles(w_i)], axis=-1)).astype(BF16)


def _trunk(x, p):
    bsz, t, d = x.shape
    assert bsz == ROWS_PER_STEP and d == D_MODEL and (t * bsz) % max(MIX_ROWS, MIX_BWD_ROWS) == 0
    depth = p["w_in"].shape[0]
    for l in range(depth):
        x = _ffn(x, l, p["ffn1_norm"], p["ffn1_w_gate"], p["ffn1_w_up"], p["ffn1_w_down"],
                 batch_major_in=(l == 0))
        xc, hb = _mix_bwd(x, l, p["mix_norm"], p["w_in"], p["lru_conv_w"], p["lru_conv_b"],
                          p["gate_w"], p["gate_b"], p["lru_lambda"])
        x = _mix_fwd(x, xc, hb, l, p["mix_norm"], p["w_in"], p["sc_conv_w"],
                     p["gate_w"], p["gate_b"], p["lru_lambda"], p["w_out"])
        last = l == depth - 1
        x = _ffn(x, l, p["ffn2_norm"], p["ffn2_w_gate"], p["ffn2_w_up"], p["ffn2_w_down"],
                 final_g=p["final_norm"] if last else None, batch_major_out=last)
    return x


def kernel(x_prompt, x_sample, ffn1_norm, ffn1_w_gate, ffn1_w_up, ffn1_w_down, mix_norm, w_in, lru_conv_w, lru_conv_b, lru_w_r, lru_b_r, lru_w_i, lru_b_i, lru_lambda, sc_conv_w, w_out, ffn2_norm, ffn2_w_gate, ffn2_w_up, ffn2_w_down, final_norm):
    depth = w_in.shape[0]
    p = dict(
        ffn1_norm=ffn1_norm.reshape(depth, 1, D_MODEL),
        ffn1_w_gate=ffn1_w_gate.astype(BF16), ffn1_w_up=ffn1_w_up.astype(BF16),
        ffn1_w_down=ffn1_w_down.astype(BF16),
        mix_norm=mix_norm.reshape(depth, 1, D_MODEL),
        w_in=w_in.astype(BF16),
        lru_conv_w=lru_conv_w, lru_conv_b=lru_conv_b.reshape(depth, 1, D_A),
        gate_w=jax.vmap(jax.vmap(_gate_tiles))(lru_w_r, lru_w_i),
        gate_b=0.5 * jnp.stack([lru_b_r, lru_b_i], axis=2),
        lru_lambda=lru_lambda.reshape(depth, 2, 1, D_A),
        sc_conv_w=sc_conv_w, w_out=w_out.astype(BF16),
        ffn2_norm=ffn2_norm.reshape(depth, 1, D_MODEL),
        ffn2_w_gate=ffn2_w_gate.astype(BF16), ffn2_w_up=ffn2_w_up.astype(BF16),
        ffn2_w_down=ffn2_w_down.astype(BF16),
        final_norm=final_norm.reshape(1, D_MODEL),
    )
    return (_trunk(x_prompt, p), _trunk(x_sample, p))
```
